```python
import jax, jax.numpy as jnp
from jax import lax
import numpy as np

D_MODEL = 1024
BATCH = 2
SEQ = 8192
DEPTH = 2

ATT_WIDTH = D_MODEL // 2
ATT_HEADS = 8
HEAD_DIM = ATT_WIDTH // ATT_HEADS
Q_BLOCK = 128
GMLP_WIDTH = D_MODEL // 4
GMLP_GROUPS = 4
GMLP_GROUP_DIM = GMLP_WIDTH // GMLP_GROUPS
CHUNK = 128
CONV_WIDTH = D_MODEL // 4
CONV_GROUPS = 4
CONV_K = 3
MIX_WIDTH = ATT_WIDTH + GMLP_WIDTH + CONV_WIDTH
SPLIT_SIZES = (ATT_WIDTH, ATT_WIDTH, ATT_WIDTH, ATT_HEADS,
               GMLP_WIDTH, GMLP_WIDTH,
               CONV_WIDTH, CONV_WIDTH, CONV_WIDTH)
IN_COLS = sum(SPLIT_SIZES)

PEER_HEADS = 8
N_KEYS = 128
N_EXPERTS = N_KEYS * N_KEYS
PEER_TOPK = 16
D_KEY = 256
D_HALF = D_KEY // 2
PEER_CHUNK = 128

EPS = 1e-6

kernel_name = "hybrid_fox_gmlp_conv_peer"


def rmsnorm(x, g):
    x32 = x.astype(jnp.float32)
    y = x32 * lax.rsqrt(jnp.mean(x32 * x32, axis=-1, keepdims=True) + EPS)
    return (y * g.astype(jnp.float32)).astype(x.dtype)


def forgetting_attention(q, k, v, cum_logf):
    Bn, S, H, dh = q.shape
    nq = S // Q_BLOCK
    scale = dh ** -0.5
    kh = k.transpose(0, 2, 1, 3)
    vh = v.transpose(0, 2, 1, 3)
    Fk = cum_logf.transpose(0, 2, 1)
    qb = q.reshape(Bn, nq, Q_BLOCK, H, dh).transpose(1, 0, 3, 2, 4)
    Fq = Fk.reshape(Bn, H, nq, Q_BLOCK).transpose(2, 0, 1, 3)
    key_pos = jnp.arange(S)

    def block(args):
        q_blk, fq_blk, i = args
        q_pos = i * Q_BLOCK + jnp.arange(Q_BLOCK)
        logits = jnp.einsum('bhqd,bhkd->bhqk', q_blk, kh).astype(jnp.float32) * scale
        logits = logits + fq_blk[..., :, None] - Fk[..., None, :]
        mask = key_pos[None, :] <= q_pos[:, None]
        logits = jnp.where(mask, logits, -1e30)
        p = jax.nn.softmax(logits, axis=-1).astype(vh.dtype)
        return jnp.einsum('bhqk,bhkd->bhqd', p, vh)

    out = lax.map(block, (qb, Fq, jnp.arange(nq)))
    return out.transpose(1, 0, 3, 2, 4).reshape(Bn, S, H * dh)


def chunked_gmlp(zu, zv, sgu_g, w_spatial, b_spatial):
    Bn, S, _ = zu.shape
    u = jax.nn.gelu(zu)
    v = jax.nn.gelu(zv).reshape(Bn, S, GMLP_GROUPS, GMLP_GROUP_DIM)
    v = rmsnorm(v, sgu_g.reshape(GMLP_GROUPS, GMLP_GROUP_DIM))
    v = v.reshape(Bn, S // CHUNK, CHUNK, GMLP_GROUPS, GMLP_GROUP_DIM)
    tril = jnp.tril(jnp.ones((CHUNK, CHUNK), dtype=bool))
    w = jnp.where(tril[None], w_spatial, jnp.zeros_like(w_spatial))
    mixed = jnp.einsum('gts,bcsgd->bctgd', w, v) + b_spatial.T[:, :, None]
    return u * mixed.reshape(Bn, S, GMLP_WIDTH)


def causal_shift(x, n):
    return jnp.pad(x, ((0, 0), (n, 0), (0, 0)))[:, :x.shape[1], :]


def short_gated_conv(h_in, gate_b, gate_c, conv_w):
    z = gate_c * h_in
    y = conv_w[0] * causal_shift(z, 2) + conv_w[1] * causal_shift(z, 1) + conv_w[2] * z
    return gate_b * y


def peer_ffn(h, w_query, sub_keys, expert_u, expert_v):
    Bn, S, D = h.shape
    T = Bn * S
    ht = h.reshape(T, D)
    q = (ht @ w_query).reshape(T, PEER_HEADS, D_KEY)
    s1 = jnp.einsum('thd,kd->thk', q[..., :D_HALF], sub_keys[0]).astype(jnp.float32)
    s2 = jnp.einsum('thd,kd->thk', q[..., D_HALF:], sub_keys[1]).astype(jnp.float32)
    v1, i1 = lax.top_k(s1, PEER_TOPK)
    v2, i2 = lax.top_k(s2, PEER_TOPK)
    cand = (v1[..., :, None] + v2[..., None, :]).reshape(T, PEER_HEADS, PEER_TOPK * PEER_TOPK)
    sc, ci = lax.top_k(cand, PEER_TOPK)
    e_idx = (jnp.take_along_axis(i1, ci // PEER_TOPK, axis=-1) * N_KEYS
             + jnp.take_along_axis(i2, ci % PEER_TOPK, axis=-1))
    gate = jax.nn.softmax(sc, axis=-1).astype(h.dtype)
    nc = T // PEER_CHUNK

    def chunk(args):
        xc, ec, gc = args
        a = jnp.einsum('chkd,cd->chk', expert_u[ec], xc)
        w = jax.nn.gelu(a) * gc
        return jnp.einsum('chk,chkd->cd', w, expert_v[ec])

    out = lax.map(chunk, (ht.reshape(nc, PEER_CHUNK, D),
                          e_idx.reshape(nc, PEER_CHUNK, PEER_HEADS, PEER_TOPK),
                          gate.reshape(nc, PEER_CHUNK, PEER_HEADS, PEER_TOPK)))
    return out.reshape(Bn, S, D)


def setup_inputs(seed: int = 0) -> dict:
    key = jax.random.key(seed)
    ks = jax.random.split(key, 16)
    f32 = jnp.float32
    nrm = lambda k, shp: jax.random.normal(k, shp, dtype=f32)
    return {
        "x": nrm(ks[0], (BATCH, SEQ, D_MODEL)),
        "norm1_g": 1.0 + 0.05 * nrm(ks[1], (DEPTH, D_MODEL)),
        "w_in": nrm(ks[2], (DEPTH, D_MODEL, IN_COLS)) * D_MODEL ** -0.5,
        "b_forget": 3.0 + 0.5 * nrm(ks[3], (DEPTH, ATT_HEADS)),
        "sgu_norm_g": 1.0 + 0.05 * nrm(ks[4], (DEPTH, GMLP_WIDTH)),
        "w_spatial": nrm(ks[5], (DEPTH, GMLP_GROUPS, CHUNK, CHUNK)) * CHUNK ** -0.5,
        "b_spatial": 1.0 + 0.1 * nrm(ks[6], (DEPTH, GMLP_GROUPS, CHUNK)),
        "conv_w": nrm(ks[7], (DEPTH, CONV_K, CONV_WIDTH)) * CONV_K ** -0.5,
        "mix_norm_g": 1.0 + 0.05 * nrm(ks[8], (DEPTH, MIX_WIDTH)),
        "w_out": nrm(ks[9], (DEPTH, MIX_WIDTH, D_MODEL)) * MIX_WIDTH ** -0.5,
        "norm2_g": 1.0 + 0.05 * nrm(ks[10], (DEPTH, D_MODEL)),
        "w_query": nrm(ks[11], (DEPTH, D_MODEL, PEER_HEADS * D_KEY)) * D_MODEL ** -0.5,
        "sub_keys": nrm(ks[12], (DEPTH, 2, N_KEYS, D_HALF)) * D_HALF ** -0.5,
        "expert_u": nrm(ks[13], (DEPTH, N_EXPERTS, D_MODEL)) * D_MODEL ** -0.5,
        "expert_v": nrm(ks[14], (DEPTH, N_EXPERTS, D_MODEL)) * PEER_HEADS ** -0.5,
        "final_g": 1.0 + 0.05 * nrm(ks[15], (D_MODEL,)),
    }


def reference(x, norm1_g, w_in, b_forget, sgu_norm_g, w_spatial, b_spatial, conv_w,
              mix_norm_g, w_out, norm2_g, w_query, sub_keys, expert_u, expert_v, final_g):
    Bn, S, _ = x.shape
    split_at = [int(s) for s in np.cumsum(SPLIT_SIZES)[:-1]]
    for l in range(DEPTH):
        h = rmsnorm(x, norm1_g[l])
        z = h @ w_in[l]
        zq, zk, zv, zf, gu, gv, c_in, c_b, c_c = jnp.split(z, split_at, axis=-1)
        logf = jax.nn.log_sigmoid((zf + b_forget[l]).astype(jnp.float32))
        cum_logf = jnp.cumsum(logf, axis=1)
        shp = (Bn, S, ATT_HEADS, HEAD_DIM)
        o_att = forgetting_attention(zq.reshape(shp), zk.reshape(shp), zv.reshape(shp), cum_logf)
        o_gmlp = chunked_gmlp(gu, gv, sgu_norm_g[l], w_spatial[l], b_spatial[l])
        o_conv = short_gated_conv(c_in, c_b, c_c, conv_w[l])
        g = mix_norm_g[l]
        mixed = jnp.concatenate([
            rmsnorm(o_att, g[:ATT_WIDTH]),
            rmsnorm(o_gmlp, g[ATT_WIDTH:ATT_WIDTH + GMLP_WIDTH]),
            rmsnorm(o_conv, g[ATT_WIDTH + GMLP_WIDTH:]),
        ], axis=-1)
        x = x + mixed @ w_out[l]
        h2 = rmsnorm(x, norm2_g[l])
        x = x + peer_ffn(h2, w_query[l], sub_keys[l], expert_u[l], expert_v[l])
    return rmsnorm(x, final_g)
```

```python
import functools

import jax
import jax.numpy as jnp
from jax import lax
from jax.experimental import pallas as pl
from jax.experimental.pallas import tpu as pltpu

f32 = jnp.float32
bf16 = jnp.bfloat16

D_MODEL = 1024
ATT_WIDTH = 512
ATT_HEADS = 8
HEAD_DIM = 64
GMLP_WIDTH = 256
GMLP_GROUPS = 4
GROUP_DIM = 64
CHUNK = 128
CONV_WIDTH = 256
CONV_K = 3
MIX_WIDTH = 1024
PEER_HEADS = 8
N_KEYS = 128
PEER_TOPK = 16
D_KEY = 256
D_HALF = 128
EPS = 1e-6
LANES = 128
NEG_BIG = -1e30

MAIN_COLS = 3 * ATT_WIDTH + 2 * GMLP_WIDTH + 3 * CONV_WIDTH

VMEM_LIMIT = 48 * 1024 * 1024

_NT = (((1,), (1,)), ((), ()))


def _cparams(sem):
    return pltpu.CompilerParams(dimension_semantics=sem, vmem_limit_bytes=VMEM_LIMIT)


def _gelu(x):
    z2 = (2.0 * 0.7978845608028654) * (x + 0.044715 * (x * x * x))
    return x / (1.0 + jnp.exp(-z2))


def _rms(x, g):
    return x * lax.rsqrt(jnp.mean(x * x, axis=-1, keepdims=True) + EPS) * g


def _split3(x):
    hi = x.astype(bf16)
    r = x - hi.astype(f32)
    mid = r.astype(bf16)
    lo = (r - mid.astype(f32)).astype(bf16)
    return hi, mid, lo


def _inproj_kernel(x_ref, g_ref, w_ref, wfh_ref, wfl_ref, bf_ref,
                   qkv_ref, gm_ref, cv_ref, logf_ref):
    h = _rms(x_ref[...], g_ref[...])
    hb = h.astype(bf16)
    z = jnp.dot(hb, w_ref[...], preferred_element_type=f32)
    qkv_ref[:, :ATT_WIDTH] = (z[:, :ATT_WIDTH] * (HEAD_DIM ** -0.5)).astype(bf16)
    qkv_ref[:, ATT_WIDTH:] = z[:, ATT_WIDTH:3 * ATT_WIDTH].astype(bf16)
    gm_ref[...] = z[:, 3 * ATT_WIDTH:3 * ATT_WIDTH + 2 * GMLP_WIDTH]
    cv_ref[...] = z[:, 3 * ATT_WIDTH + 2 * GMLP_WIDTH:]
    hl = (h - hb.astype(f32)).astype(bf16)
    zf = (jnp.dot(hb, wfh_ref[...], preferred_element_type=f32)
          + jnp.dot(hb, wfl_ref[...], preferred_element_type=f32)
          + jnp.dot(hl, wfh_ref[...], preferred_element_type=f32)) + bf_ref[...]
    logf_ref[...] = jnp.minimum(zf, 0.0) - jnp.log1p(jnp.exp(-jnp.abs(zf)))


def _in_proj(x2, g, w, wfh, wfl, bfp, tb):
    T = x2.shape[0]
    const = lambda i: (0, 0)
    return pl.pallas_call(
        _inproj_kernel,
        grid=(T // tb,),
        in_specs=[pl.BlockSpec((tb, D_MODEL), lambda i: (i, 0)),
                  pl.BlockSpec((1, D_MODEL), const),
                  pl.BlockSpec((D_MODEL, MAIN_COLS), const),
                  pl.BlockSpec((D_MODEL, LANES), const),
                  pl.BlockSpec((D_MODEL, LANES), const),
                  pl.BlockSpec((1, LANES), const)],
        out_specs=[pl.BlockSpec((tb, 3 * ATT_WIDTH), lambda i: (i, 0)),
                   pl.BlockSpec((tb, 2 * GMLP_WIDTH), lambda i: (i, 0)),
                   pl.BlockSpec((tb, 3 * CONV_WIDTH), lambda i: (i, 0)),
                   pl.BlockSpec((tb, LANES), lambda i: (i, 0))],
        out_shape=[jax.ShapeDtypeStruct((T, 3 * ATT_WIDTH), bf16),
                   jax.ShapeDtypeStruct((T, 2 * GMLP_WIDTH), f32),
                   jax.ShapeDtypeStruct((T, 3 * CONV_WIDTH), f32),
                   jax.ShapeDtypeStruct((T, LANES), f32)],
        compiler_params=_cparams(("parallel",)),
        name="in_proj",
    )(x2, g, w, wfh, wfl, bfp)


def _cumsum_kernel(x_ref, o_ref, carry_ref, *, tc):
    @pl.when(pl.program_id(1) == 0)
    def _():
        carry_ref[...] = jnp.zeros_like(carry_ref)

    row = lax.broadcasted_iota(jnp.int32, (tc, tc), 0)
    col = lax.broadcasted_iota(jnp.int32, (tc, tc), 1)
    tri = jnp.where(col <= row, 1.0, 0.0).astype(bf16)
    hi, mid, lo = _split3(x_ref[...])
    c = (jnp.dot(tri, hi, preferred_element_type=f32)
         + jnp.dot(tri, mid, preferred_element_type=f32)
         + jnp.dot(tri, lo, preferred_element_type=f32)) + carry_ref[...]
    o_ref[...] = c
    carry_ref[...] = c[tc - 1:tc, :]


def _cumsum(logf, B, S, tc):
    nb = S // tc
    return pl.pallas_call(
        functools.partial(_cumsum_kernel, tc=tc),
        grid=(B, nb),
        in_specs=[pl.BlockSpec((tc, LANES), lambda b, j: (b * nb + j, 0))],
        out_specs=pl.BlockSpec((tc, LANES), lambda b, j: (b * nb + j, 0)),
        out_shape=jax.ShapeDtypeStruct(logf.shape, f32),
        scratch_shapes=[pltpu.VMEM((1, LANES), f32)],
        compiler_params=_cparams(("arbitrary", "arbitrary")),
        name="cumsum_logf",
    )(logf)


def _attn_kernel(q_ref, k_ref, v_ref, f_ref, o_ref, acc_ref, m_ref, l_ref, *, tq):
    i = pl.program_id(2)
    lane = lax.broadcasted_iota(jnp.int32, (tq, LANES), 1)
    q = q_ref[...]
    zero = jnp.zeros_like(q)
    qs = (jnp.where(lane < HEAD_DIM, q, zero), jnp.where(lane >= HEAD_DIM, q, zero))
    m_ref[...] = jnp.full(m_ref.shape, NEG_BIG, f32)
    l_ref[...] = jnp.zeros(l_ref.shape, f32)
    acc_ref[...] = jnp.zeros(acc_ref.shape, f32)

    def step(j, masked):
        start = pl.multiple_of(j * tq, tq)
        k = k_ref[pl.ds(start, tq), :]
        v = v_ref[pl.ds(start, tq), :]
        fk = f_ref[0, 0, :, pl.ds(start, tq)]
        for hh in range(2):
            s = lax.dot_general(qs[hh], k, _NT, preferred_element_type=f32)
            s = s - fk[hh:hh + 1, :]
            if masked:
                r = lax.broadcasted_iota(jnp.int32, (tq, tq), 0)
                c = lax.broadcasted_iota(jnp.int32, (tq, tq), 1)
                s = jnp.where(c <= r, s, NEG_BIG)
            m_prev = m_ref[hh]
            m_new = jnp.maximum(m_prev, jnp.max(s, axis=1, keepdims=True))
            alpha = jnp.exp(m_prev - m_new)
            p = jnp.exp(s - m_new)
            l_ref[hh] = alpha * l_ref[hh] + jnp.sum(p, axis=1, keepdims=True)
            acc_ref[hh] = alpha * acc_ref[hh] + jnp.dot(p.astype(bf16), v,
                                                        preferred_element_type=f32)
            m_ref[hh] = m_new

    def body(j, carry):
        step(j, False)
        return carry

    lax.fori_loop(0, i, body, 0)
    step(i, True)
    o_a = acc_ref[0] / l_ref[0]
    o_b = acc_ref[1] / l_ref[1]
    o_ref[...] = jnp.where(lane < HEAD_DIM, o_a, o_b)


def _attention(qkv, fk, B, S, tq):
    T = B * S
    nq = S // tq
    npair = ATT_HEADS // 2
    return pl.pallas_call(
        functools.partial(_attn_kernel, tq=tq),
        grid=(B, npair, nq),
        in_specs=[pl.BlockSpec((tq, LANES), lambda b, p, i: (b * nq + i, p)),
                  pl.BlockSpec((S, LANES), lambda b, p, i: (b, npair + p)),
                  pl.BlockSpec((S, LANES), lambda b, p, i: (b, 2 * npair + p)),
                  pl.BlockSpec((1, 1, 2, S), lambda b, p, i: (b, p, 0, 0))],
        out_specs=pl.BlockSpec((tq, LANES), lambda b, p, i: (b * nq + i, p)),
        out_shape=jax.ShapeDtypeStruct((T, ATT_WIDTH), f32),
        scratch_shapes=[pltpu.VMEM((2, tq, LANES), f32),
                        pltpu.VMEM((2, tq, 1), f32),
                        pltpu.VMEM((2, tq, 1), f32)],
        compiler_params=_cparams(("parallel", "parallel", "arbitrary")),
        name="fox_attention",
    )(qkv, qkv, qkv, fk)


def _mixout_kernel(oatt_ref, gm_ref, cv_ref, cvp_ref, x_ref, sgug_ref, wsp_ref, bsp_ref,
                   cw_ref, mg_ref, wout_ref, n2g_ref, xo_ref, h2_ref, *, tb, blocks_per_seq):
    i = pl.program_id(0)
    gm = gm_ref[...]
    u = _gelu(gm[:, :GMLP_WIDTH])
    v = _gelu(gm[:, GMLP_WIDTH:])
    gr = lax.broadcasted_iota(jnp.int32, (GMLP_WIDTH, GMLP_WIDTH), 0) // GROUP_DIM
    gc = lax.broadcasted_iota(jnp.int32, (GMLP_WIDTH, GMLP_WIDTH), 1) // GROUP_DIM
    grp = jnp.where(gr == gc, 1.0 / GROUP_DIM, 0.0).astype(bf16)
    vh, vm, vl = _split3(v * v)
    ms = (jnp.dot(vh, grp, preferred_element_type=f32)
          + jnp.dot(vm, grp, preferred_element_type=f32)
          + jnp.dot(vl, grp, preferred_element_type=f32))
    vn = (v * lax.rsqrt(ms + EPS) * sgug_ref[...]).astype(bf16)
    tr = lax.broadcasted_iota(jnp.int32, (CHUNK, CHUNK), 0)
    tc = lax.broadcasted_iota(jnp.int32, (CHUNK, CHUNK), 1)
    lane_g = lax.broadcasted_iota(jnp.int32, (CHUNK, GMLP_WIDTH), 1) // GROUP_DIM
    mixed_chunks = []
    for c in range(tb // CHUNK):
        vc = vn[c * CHUNK:(c + 1) * CHUNK, :]
        mixed = jnp.zeros((CHUNK, GMLP_WIDTH), f32)
        for g in range(GMLP_GROUPS):
            wg = jnp.where(tc <= tr, wsp_ref[g], 0.0).astype(bf16)
            mg = jnp.dot(wg, vc, preferred_element_type=f32)
            mixed = jnp.where(lane_g == g, mg, mixed)
        mixed_chunks.append(mixed + bsp_ref[...])
    o_gmlp = u * jnp.concatenate(mixed_chunks, axis=0)
    cv = cv_ref[...]
    c_in = cv[:, :CONV_WIDTH]
    c_b = cv[:, CONV_WIDTH:2 * CONV_WIDTH]
    c_c = cv[:, 2 * CONV_WIDTH:]
    z = c_c * c_in
    cvp = cvp_ref[...]
    zp = cvp[:, 2 * CONV_WIDTH:] * cvp[:, :CONV_WIDTH]
    zp = jnp.where(i % blocks_per_seq == 0, jnp.zeros_like(zp), zp)
    rowi = lax.broadcasted_iota(jnp.int32, (tb, CONV_WIDTH), 0)
    z1 = pltpu.roll(z, 1, axis=0)
    z1 = jnp.where(rowi == 0, zp[7:8, :], z1)
    z2 = pltpu.roll(z, 2, axis=0)
    z2 = jnp.where(rowi == 0, zp[6:7, :], z2)
    z2 = jnp.where(rowi == 1, zp[7:8, :], z2)
    cw = cw_ref[...]
    o_conv = c_b * (cw[0:1, :] * z2 + cw[1:2, :] * z1 + cw[2:3, :] * z)
    mg_ = mg_ref[...]
    mixed_all = jnp.concatenate([
        _rms(oatt_ref[...], mg_[:, :ATT_WIDTH]),
        _rms(o_gmlp, mg_[:, ATT_WIDTH:ATT_WIDTH + GMLP_WIDTH]),
        _rms(o_conv, mg_[:, ATT_WIDTH + GMLP_WIDTH:])], axis=-1).astype(bf16)
    xn = x_ref[...] + jnp.dot(mixed_all, wout_ref[...], preferred_element_type=f32)
    xo_ref[...] = xn
    h2_ref[...] = _rms(xn, n2g_ref[...]).astype(bf16)


def _mix_out(oatt, gm, cv, x2, sgug, wsp, bspf, cw, mg, wout, n2g, S, tb):
    T = x2.shape[0]
    bps = S // tb
    const2 = lambda i: (0, 0)
    row = lambda i: (i, 0)
    return pl.pallas_call(
        functools.partial(_mixout_kernel, tb=tb, blocks_per_seq=bps),
        grid=(T // tb,),
        in_specs=[pl.BlockSpec((tb, ATT_WIDTH), row),
                  pl.BlockSpec((tb, 2 * GMLP_WIDTH), row),
                  pl.BlockSpec((tb, 3 * CONV_WIDTH), row),
                  pl.BlockSpec((8, 3 * CONV_WIDTH),
                               lambda i: (jnp.maximum(i * (tb // 8) - 1, 0), 0)),
                  pl.BlockSpec((tb, D_MODEL), row),
                  pl.BlockSpec((1, GMLP_WIDTH), const2),
                  pl.BlockSpec((GMLP_GROUPS, CHUNK, CHUNK), lambda i: (0, 0, 0)),
                  pl.BlockSpec((CHUNK, GMLP_WIDTH), const2),
                  pl.BlockSpec((8, CONV_WIDTH), const2),
                  pl.BlockSpec((1, MIX_WIDTH), const2),
                  pl.BlockSpec((MIX_WIDTH, D_MODEL), const2),
                  pl.BlockSpec((1, D_MODEL), const2)],
        out_specs=[pl.BlockSpec((tb, D_MODEL), row),
                   pl.BlockSpec((tb, D_MODEL), row)],
        out_shape=[jax.ShapeDtypeStruct((T, D_MODEL), f32),
                   jax.ShapeDtypeStruct((T, D_MODEL), bf16)],
        compiler_params=_cparams(("parallel",)),
        name="mix_out",
    )(oatt, gm, cv, cv, x2, sgug, wsp, bspf, cw, mg, wout, n2g)


def _colmax(x):
    return jnp.max(x, axis=0, keepdims=True)


def _top16(s, vals_ref):
    rank = jnp.full(s.shape, float(PEER_TOPK), f32)
    for a in range(PEER_TOPK):
        m = _colmax(s)
        hit = s == m
        rank = jnp.where(hit, float(a), rank)
        s = jnp.where(hit, -jnp.inf, s)
        vals_ref[a:a + 1, :] = m
    return rank


def _peerprep_kernel(h2_ref, wq_ref, keys_ref, r2_ref, e2_ref, n1_ref, c1_ref,
                     v1_ref, v2_ref):
    qT = lax.dot_general(wq_ref[...], h2_ref[...], _NT, preferred_element_type=f32)
    qb = qT.astype(bf16)
    s1 = jnp.dot(keys_ref[0], qb[:D_HALF], preferred_element_type=f32)
    s2 = jnp.dot(keys_ref[1], qb[D_HALF:], preferred_element_type=f32)
    r1 = _top16(s1, v1_ref)
    r2 = _top16(s2, v2_ref)
    v1 = v1_ref[...]
    v2 = v2_ref[...]
    cands = [v1[0:1] + v2]
    cands += [v1[a:a + 1] + v2[0:8] for a in range(1, 8)]
    cands += [v1[8:16] + v2[0:1]]
    cmax = v1[0:1] + v2[0:1]
    work = list(cands)
    tau = cmax
    cnt = jnp.zeros_like(cmax)
    for _ in range(PEER_TOPK):
        m = _colmax(functools.reduce(jnp.maximum, work[1:], work[0][0:8]))
        m = jnp.maximum(m, _colmax(work[0][8:16]))
        tau = jnp.where(cnt < float(PEER_TOPK), m, tau)
        hits = [w == m for w in work]
        cnt = cnt + sum(jnp.sum(jnp.where(h, 1.0, 0.0), axis=0, keepdims=True) for h in hits)
        work = [jnp.where(h, -jnp.inf, w) for h, w in zip(hits, work)]
    sel = [c >= tau for c in cands]
    zsum = sum(jnp.sum(jnp.where(s_, jnp.exp(c - cmax), 0.0), axis=0, keepdims=True)
               for s_, c in zip(sel, cands))
    n_lo = [jnp.sum(jnp.where(s_, 1.0, 0.0), axis=0, keepdims=True) for s_ in sel[:8]]
    n_hi = jnp.where(sel[8], 1.0, 0.0)
    n1 = jnp.zeros(r1.shape, f32)
    for a in range(8):
        n1 = jnp.where(r1 == float(a), n_lo[a], n1)
    for a in range(8, PEER_TOPK):
        n1 = jnp.where(r1 == float(a), n_hi[a - 8:a - 7], n1)
    r2_ref[0] = r2
    e2_ref[0] = jnp.exp(s2 - v2[0:1])
    n1_ref[0] = n1
    c1_ref[0] = jnp.exp(s1 - v1[0:1]) / zsum


def _peer_prep(h2, wqT, keys, tb):
    T = h2.shape[0]
    out = jax.ShapeDtypeStruct((PEER_HEADS, N_KEYS, T), f32)
    ospec = pl.BlockSpec((1, N_KEYS, tb), lambda t, h: (h, 0, t))
    return pl.pallas_call(
        _peerprep_kernel,
        grid=(T // tb, PEER_HEADS),
        in_specs=[pl.BlockSpec((tb, D_MODEL), lambda t, h: (t, 0)),
                  pl.BlockSpec((D_KEY, D_MODEL), lambda t, h: (h, 0)),
                  pl.BlockSpec((2, N_KEYS, D_HALF), lambda t, h: (0, 0, 0))],
        out_specs=[ospec, ospec, ospec, ospec],
        out_shape=[out, out, out, out],
        scratch_shapes=[pltpu.VMEM((PEER_TOPK, tb), f32), pltpu.VMEM((PEER_TOPK, tb), f32)],
        compiler_params=_cparams(("parallel", "arbitrary")),
        name="peer_prep",
    )(h2, wqT, keys)


def _peermain_kernel(h2_ref, u_ref, vt_ref, r2_ref, e2_ref, n1_ref, c1_ref, x_ref,
                     o_ref, acc_ref, w_ref, *, rows):
    e = pl.program_id(1)

    @pl.when(e == 0)
    def _():
        acc_ref[...] = jnp.zeros_like(acc_ref)

    aT = lax.dot_general(u_ref[...], h2_ref[...], _NT, preferred_element_type=f32)
    for r in range(rows):
        i1 = e * rows + r
        g = jnp.zeros((N_KEYS, aT.shape[1]), f32)
        for h in range(PEER_HEADS):
            n1row = n1_ref[h, pl.ds(i1, 1), :]
            c1row = c1_ref[h, pl.ds(i1, 1), :]
            g = g + jnp.where(r2_ref[h] < n1row, e2_ref[h], 0.0) * c1row
        w_ref[r * N_KEYS:(r + 1) * N_KEYS, :] = (
            _gelu(aT[r * N_KEYS:(r + 1) * N_KEYS, :]) * g).astype(bf16)
    acc_ref[...] += jnp.dot(vt_ref[...], w_ref[...], preferred_element_type=f32)

    @pl.when(e == pl.num_programs(1) - 1)
    def _():
        o_ref[...] = x_ref[...] + acc_ref[...].T


def _peer_main(h2, u, vt, r2, e2, n1, c1, x2, tb, rows):
    T = h2.shape[0]
    eb = rows * N_KEYS
    sel = pl.BlockSpec((PEER_HEADS, N_KEYS, tb), lambda t, e: (0, 0, t))
    return pl.pallas_call(
        functools.partial(_peermain_kernel, rows=rows),
        grid=(T // tb, N_KEYS // rows),
        in_specs=[pl.BlockSpec((tb, D_MODEL), lambda t, e: (t, 0)),
                  pl.BlockSpec((eb, D_MODEL), lambda t, e: (e, 0)),
                  pl.BlockSpec((D_MODEL, eb), lambda t, e: (0, e)),
                  sel, sel, sel, sel,
                  pl.BlockSpec((tb, D_MODEL), lambda t, e: (t, 0))],
        out_specs=pl.BlockSpec((tb, D_MODEL), lambda t, e: (t, 0)),
        out_shape=jax.ShapeDtypeStruct((T, D_MODEL), f32),
        scratch_shapes=[pltpu.VMEM((D_MODEL, tb), f32), pltpu.VMEM((eb, tb), bf16)],
        compiler_params=_cparams(("parallel", "arbitrary")),
        name="peer_main",
    )(h2, u, vt, r2, e2, n1, c1, x2)


def _finalnorm_kernel(x_ref, g_ref, o_ref):
    o_ref[...] = _rms(x_ref[...], g_ref[...])


def _final_norm(x2, g, tb):
    T = x2.shape[0]
    return pl.pallas_call(
        _finalnorm_kernel,
        grid=(T // tb,),
        in_specs=[pl.BlockSpec((tb, D_MODEL), lambda i: (i, 0)),
                  pl.BlockSpec((1, D_MODEL), lambda i: (0, 0))],
        out_specs=pl.BlockSpec((tb, D_MODEL), lambda i: (i, 0)),
        out_shape=jax.ShapeDtypeStruct(x2.shape, f32),
        compiler_params=_cparams(("parallel",)),
        name="final_norm",
    )(x2, g)


def _block_sizes(S):
    return dict(tb_in=min(512, S), tc=min(512, S), tq=min(256, S), tb_mix=min(256, S),
                tb_prep=min(256, S), tb_peer=min(256, S), rows=8)


def kernel(x, norm1_g, w_in, b_forget, sgu_norm_g, w_spatial, b_spatial, conv_w,
           mix_norm_g, w_out, norm2_g, w_query, sub_keys, expert_u, expert_v, final_g):
    B, S, D = x.shape
    T = B * S
    depth = w_in.shape[0]
    bs = _block_sizes(S)
    x2 = x.reshape(T, D)
    f0 = 3 * ATT_WIDTH
    for l in range(depth):
        wl = w_in[l]
        w_main = jnp.concatenate([wl[:, :f0], wl[:, f0 + ATT_HEADS:]], axis=1).astype(bf16)
        wf = jnp.pad(wl[:, f0:f0 + ATT_HEADS], ((0, 0), (0, LANES - ATT_HEADS)))
        wfh = wf.astype(bf16)
        wfl = (wf - wfh.astype(f32)).astype(bf16)
        bfp = jnp.pad(b_forget[l], (0, LANES - ATT_HEADS)).reshape(1, LANES)
        qkv, gm, cv, logf = _in_proj(x2, norm1_g[l].reshape(1, D), w_main, wfh, wfl, bfp,
                                     bs["tb_in"])
        cum = _cumsum(logf, B, S, bs["tc"])
        fk = cum[:, :ATT_HEADS].reshape(B, S, ATT_HEADS // 2, 2).transpose(0, 2, 3, 1)
        oatt = _attention(qkv, fk, B, S, bs["tq"])
        bspf = jnp.repeat(b_spatial[l].T, GROUP_DIM, axis=1)
        cw = jnp.pad(conv_w[l], ((0, 8 - CONV_K), (0, 0)))
        x2, h2 = _mix_out(oatt, gm, cv, x2, sgu_norm_g[l].reshape(1, GMLP_WIDTH), w_spatial[l],
                          bspf, cw, mix_norm_g[l].reshape(1, MIX_WIDTH), w_out[l].astype(bf16),
                          norm2_g[l].reshape(1, D), S, bs["tb_mix"])
        r2, e2, n1, c1 = _peer_prep(h2, w_query[l].T.astype(bf16), sub_keys[l].astype(bf16),
                                    bs["tb_prep"])
        x2 = _peer_main(h2, expert_u[l].astype(bf16), expert_v[l].T.astype(bf16),
                        r2, e2, n1, c1, x2, bs["tb_peer"], bs["rows"])
    return _final_norm(x2, final_g.reshape(1, D), bs["tb_in"]).reshape(B, S, D)
```

```python
import functools
import math

import jax
import jax.numpy as jnp
from jax import lax
from jax.experimental import pallas as pl
from jax.experimental.pallas import tpu as pltpu

f32 = jnp.float32
bf16 = jnp.bfloat16
u32 = jnp.uint32

D_MODEL = 1024
ATT_WIDTH = 512
ATT_HEADS = 8
HEAD_DIM = 64
GMLP_WIDTH = 256
GMLP_GROUPS = 4
GROUP_DIM = 64
CHUNK = 128
CONV_WIDTH = 256
CONV_K = 3
MIX_WIDTH = 1024
PEER_HEADS = 8
N_KEYS = 128
PEER_TOPK = 16
D_KEY = 256
D_HALF = 128
EPS = 1e-6
LANES = 128
BF16_ROWS = 16
SLAB = 256
NEG_BIG = -1e30
LOG2E = math.log2(math.e)
F_PARTS = 3
V_ROWS = 80

MAIN_COLS = 3 * ATT_WIDTH + 2 * GMLP_WIDTH + 3 * CONV_WIDTH

VMEM_LIMIT = 56 * 1024 * 1024

_NT = (((1,), (1,)), ((), ()))


def _cparams(sem):
    return pltpu.CompilerParams(dimension_semantics=sem, vmem_limit_bytes=VMEM_LIMIT)


def _gelu(x):
    z2 = (2.0 * 0.7978845608028654) * (x + 0.044715 * (x * x * x))
    return x / (1.0 + jnp.exp(-z2))


def _rms(x, g):
    return x * lax.rsqrt(jnp.mean(x * x, axis=-1, keepdims=True) + EPS) * g


def _split3(x):
    hi = x.astype(bf16)
    r = x - hi.astype(f32)
    mid = r.astype(bf16)
    lo = (r - mid.astype(f32)).astype(bf16)
    return hi, mid, lo


def _inproj_kernel(x_ref, g_ref, w_ref, wf_ref, bf_ref,
                   qkv_ref, gm_ref, cv_ref, logf_ref):
    h = _rms(x_ref[...], g_ref[...])
    hb = h.astype(bf16)
    z = jnp.dot(hb, w_ref[...], preferred_element_type=f32)
    qkv_ref[:, :ATT_WIDTH] = (z[:, :ATT_WIDTH] * (HEAD_DIM ** -0.5 * LOG2E)).astype(bf16)
    qkv_ref[:, ATT_WIDTH:] = z[:, ATT_WIDTH:3 * ATT_WIDTH].astype(bf16)
    gm_ref[...] = z[:, 3 * ATT_WIDTH:3 * ATT_WIDTH + 2 * GMLP_WIDTH]
    cv_ref[...] = z[:, 3 * ATT_WIDTH + 2 * GMLP_WIDTH:]
    hl = (h - hb.astype(f32)).astype(bf16)
    wf = wf_ref[...]
    wfh = wf.astype(bf16)
    wfl = (wf - wfh.astype(f32)).astype(bf16)
    zf = (jnp.dot(hb, wfh, preferred_element_type=f32)
          + jnp.dot(hb, wfl, preferred_element_type=f32)
          + jnp.dot(hl, wfh, preferred_element_type=f32)) + bf_ref[...]
    logf_ref[...] = jnp.minimum(zf, 0.0) - jnp.log1p(jnp.exp(-jnp.abs(zf)))


def _in_proj(x2, g, w, wf, bfp, tb):
    T = x2.shape[0]
    const = lambda i: (0, 0)
    return pl.pallas_call(
        _inproj_kernel,
        grid=(T // tb,),
        in_specs=[pl.BlockSpec((tb, D_MODEL), lambda i: (i, 0)),
                  pl.BlockSpec((1, D_MODEL), const),
                  pl.BlockSpec((D_MODEL, MAIN_COLS), const),
                  pl.BlockSpec((D_MODEL, LANES), const),
                  pl.BlockSpec((1, LANES), const)],
        out_specs=[pl.BlockSpec((tb, 3 * ATT_WIDTH), lambda i: (i, 0)),
                   pl.BlockSpec((tb, 2 * GMLP_WIDTH), lambda i: (i, 0)),
                   pl.BlockSpec((tb, 3 * CONV_WIDTH), lambda i: (i, 0)),
                   pl.BlockSpec((tb, LANES), lambda i: (i, 0))],
        out_shape=[jax.ShapeDtypeStruct((T, 3 * ATT_WIDTH), bf16),
                   jax.ShapeDtypeStruct((T, 2 * GMLP_WIDTH), f32),
                   jax.ShapeDtypeStruct((T, 3 * CONV_WIDTH), f32),
                   jax.ShapeDtypeStruct((T, LANES), f32)],
        compiler_params=_cparams(("parallel",)),
        name="in_proj",
    )(x2, g, w, wf, bfp)


def _cumsum_kernel(x_ref, hi_ref, mid_ref, lo_ref, carry_ref, *, tc):
    @pl.when(pl.program_id(1) == 0)
    def _():
        carry_ref[...] = jnp.zeros_like(carry_ref)

    row = lax.broadcasted_iota(jnp.int32, (tc, tc), 0)
    col = lax.broadcasted_iota(jnp.int32, (tc, tc), 1)
    tri = jnp.where(col <= row, 1.0, 0.0).astype(bf16)
    hi, mid, lo = _split3(x_ref[...])
    c = (jnp.dot(tri, hi, preferred_element_type=f32)
         + jnp.dot(tri, mid, preferred_element_type=f32)
         + jnp.dot(tri, lo, preferred_element_type=f32)) + carry_ref[...]
    carry_ref[...] = c[tc - 1:tc, :]
    hi_ref[...], mid_ref[...], lo_ref[...] = _split3(c * LOG2E)


def _cumsum(logf, B, S, tc):
    nb = S // tc
    spec = pl.BlockSpec((tc, LANES), lambda b, j: (b * nb + j, 0))
    return pl.pallas_call(
        functools.partial(_cumsum_kernel, tc=tc),
        grid=(B, nb),
        in_specs=[spec],
        out_specs=[spec, spec, spec],
        out_shape=[jax.ShapeDtypeStruct(logf.shape, bf16)] * F_PARTS,
        scratch_shapes=[pltpu.VMEM((1, LANES), f32)],
        compiler_params=_cparams(("arbitrary", "arbitrary")),
        name="cumsum_logf",
    )(logf)


def _attn_kernel(qa_ref, ka_ref, vt_ref, o_ref, acc_ref, m_ref, *, tq, hg):
    i = pl.program_id(2)
    m_ref[...] = jnp.full(m_ref.shape, NEG_BIG, f32)
    acc_ref[...] = jnp.zeros(acc_ref.shape, f32)

    def step(j, masked):
        start = pl.multiple_of(j * tq, tq)
        for h in range(hg):
            k = ka_ref[0, h, pl.ds(start, tq), :]
            s = lax.dot_general(k, qa_ref[0, h], _NT, preferred_element_type=f32)
            if masked:
                r = lax.broadcasted_iota(jnp.int32, (tq, tq), 0)
                c = lax.broadcasted_iota(jnp.int32, (tq, tq), 1)
                s = jnp.where(r <= c, s, NEG_BIG)
            m = m_ref[h]
            m_new = jnp.maximum(m, jnp.max(s, axis=0, keepdims=True))
            p = jnp.exp2(s - m_new).astype(bf16)
            pv = jnp.dot(vt_ref[0, h, :, pl.ds(start, tq)], p, preferred_element_type=f32)
            acc_ref[h] = jnp.exp2(m - m_new) * acc_ref[h] + pv
            m_ref[h] = m_new

    def body(j, carry):
        step(j, False)
        return carry

    lax.fori_loop(0, i, body, 0)
    step(i, True)
    for h in range(hg):
        o_ref[0, h * HEAD_DIM:(h + 1) * HEAD_DIM, :] = (
            acc_ref[h, :HEAD_DIM, :] / acc_ref[h, HEAD_DIM:HEAD_DIM + 1, :])


def _attention(qa, ka, vt, tq, hg):
    B, H, S, _ = qa.shape
    return pl.pallas_call(
        functools.partial(_attn_kernel, tq=tq, hg=hg),
        grid=(B, H // hg, S // tq),
        in_specs=[pl.BlockSpec((1, hg, tq, LANES), lambda b, g, i: (b, g, i, 0)),
                  pl.BlockSpec((1, hg, S, LANES), lambda b, g, i: (b, g, 0, 0)),
                  pl.BlockSpec((1, hg, V_ROWS, S), lambda b, g, i: (b, g, 0, 0))],
        out_specs=pl.BlockSpec((1, hg * HEAD_DIM, tq), lambda b, g, i: (b, g, i)),
        out_shape=jax.ShapeDtypeStruct((B, H * HEAD_DIM, S), f32),
        scratch_shapes=[pltpu.VMEM((hg, V_ROWS, tq), f32), pltpu.VMEM((hg, 1, tq), f32)],
        compiler_params=_cparams(("parallel", "parallel", "arbitrary")),
        name="fox_attention",
    )(qa, ka, vt)


def _attention_operands(qkv, cum_parts, B, S):
    H, dh = ATT_HEADS, HEAD_DIM
    q = qkv[:, :ATT_WIDTH].reshape(B, S, H, dh).transpose(0, 2, 1, 3)
    k = qkv[:, ATT_WIDTH:2 * ATT_WIDTH].reshape(B, S, H, dh).transpose(0, 2, 1, 3)
    v = qkv[:, 2 * ATT_WIDTH:].reshape(B, S, H, dh).transpose(0, 2, 3, 1)
    fparts = jnp.stack([p[:, :H].reshape(B, S, H).transpose(0, 2, 1) for p in cum_parts],
                       axis=-1)
    pad = jnp.zeros((B, H, S, LANES - dh - F_PARTS), bf16)
    ka = jnp.concatenate([k, fparts, pad], axis=-1)
    qa = jnp.concatenate([q, jnp.full((B, H, S, F_PARTS), -1.0, bf16), pad], axis=-1)
    vt = jnp.concatenate([v, jnp.ones((B, H, 1, S), bf16),
                          jnp.zeros((B, H, V_ROWS - dh - 1, S), bf16)], axis=2)
    return qa, ka, vt


def _mixout_kernel(oatt_ref, gm_ref, cv_ref, cvp_ref, x_ref, sgug_ref, wsp_ref, bsp_ref,
                   cw_ref, mg_ref, wout_ref, n2g_ref, xo_ref, h2_ref, *, tb, blocks_per_seq):
    i = pl.program_id(0)
    gm = gm_ref[...]
    u = _gelu(gm[:, :GMLP_WIDTH])
    v = _gelu(gm[:, GMLP_WIDTH:])
    gr = lax.broadcasted_iota(jnp.int32, (GMLP_WIDTH, GMLP_WIDTH), 0) // GROUP_DIM
    gc = lax.broadcasted_iota(jnp.int32, (GMLP_WIDTH, GMLP_WIDTH), 1) // GROUP_DIM
    grp = jnp.where(gr == gc, 1.0 / GROUP_DIM, 0.0).astype(bf16)
    vh, vm, vl = _split3(v * v)
    ms = (jnp.dot(vh, grp, preferred_element_type=f32)
          + jnp.dot(vm, grp, preferred_element_type=f32)
          + jnp.dot(vl, grp, preferred_element_type=f32))
    vn = (v * lax.rsqrt(ms + EPS) * sgug_ref[...]).astype(bf16)
    tr = lax.broadcasted_iota(jnp.int32, (CHUNK, CHUNK), 0)
    tc = lax.broadcasted_iota(jnp.int32, (CHUNK, CHUNK), 1)
    lane_g = lax.broadcasted_iota(jnp.int32, (CHUNK, GMLP_WIDTH), 1) // GROUP_DIM
    mixed_chunks = []
    for c in range(tb // CHUNK):
        vc = vn[c * CHUNK:(c + 1) * CHUNK, :]
        mixed = jnp.zeros((CHUNK, GMLP_WIDTH), f32)
        for g in range(GMLP_GROUPS):
            wg = jnp.where(tc <= tr, wsp_ref[g], 0.0).astype(bf16)
            mg = jnp.dot(wg, vc, preferred_element_type=f32)
            mixed = jnp.where(lane_g == g, mg, mixed)
        mixed_chunks.append(mixed + bsp_ref[...])
    o_gmlp = u * jnp.concatenate(mixed_chunks, axis=0)
    cv = cv_ref[...]
    c_in = cv[:, :CONV_WIDTH]
    c_b = cv[:, CONV_WIDTH:2 * CONV_WIDTH]
    c_c = cv[:, 2 * CONV_WIDTH:]
    z = c_c * c_in
    cvp = cvp_ref[...]
    zp = cvp[:, 2 * CONV_WIDTH:] * cvp[:, :CONV_WIDTH]
    zp = jnp.where(i % blocks_per_seq == 0, jnp.zeros_like(zp), zp)
    rowi = lax.broadcasted_iota(jnp.int32, (tb, CONV_WIDTH), 0)
    z1 = pltpu.roll(z, 1, axis=0)
    z1 = jnp.where(rowi == 0, zp[7:8, :], z1)
    z2 = pltpu.roll(z, 2, axis=0)
    z2 = jnp.where(rowi == 0, zp[6:7, :], z2)
    z2 = jnp.where(rowi == 1, zp[7:8, :], z2)
    cw = cw_ref[...]
    o_conv = c_b * (cw[0:1, :] * z2 + cw[1:2, :] * z1 + cw[2:3, :] * z)
    mg_ = mg_ref[...]
    o_att = oatt_ref[0].T
    mixed_all = jnp.concatenate([
        _rms(o_att, mg_[:, :ATT_WIDTH]),
        _rms(o_gmlp, mg_[:, ATT_WIDTH:ATT_WIDTH + GMLP_WIDTH]),
        _rms(o_conv, mg_[:, ATT_WIDTH + GMLP_WIDTH:])], axis=-1).astype(bf16)
    xn = x_ref[...] + jnp.dot(mixed_all, wout_ref[...], preferred_element_type=f32)
    xo_ref[...] = xn
    h2_ref[...] = _rms(xn, n2g_ref[...]).T.astype(bf16)


def _mix_out(oatt_t, gm, cv, x2, sgug, wsp, bspf, cw, mg, wout, n2g, S, tb):
    T = x2.shape[0]
    bps = S // tb
    const2 = lambda i: (0, 0)
    row = lambda i: (i, 0)
    return pl.pallas_call(
        functools.partial(_mixout_kernel, tb=tb, blocks_per_seq=bps),
        grid=(T // tb,),
        in_specs=[pl.BlockSpec((1, ATT_WIDTH, tb), lambda i: (i // bps, 0, i % bps)),
                  pl.BlockSpec((tb, 2 * GMLP_WIDTH), row),
                  pl.BlockSpec((tb, 3 * CONV_WIDTH), row),
                  pl.BlockSpec((8, 3 * CONV_WIDTH),
                               lambda i: (jnp.maximum(i * (tb // 8) - 1, 0), 0)),
                  pl.BlockSpec((tb, D_MODEL), row),
                  pl.BlockSpec((1, GMLP_WIDTH), const2),
                  pl.BlockSpec((GMLP_GROUPS, CHUNK, CHUNK), lambda i: (0, 0, 0)),
                  pl.BlockSpec((CHUNK, GMLP_WIDTH), const2),
                  pl.BlockSpec((8, CONV_WIDTH), const2),
                  pl.BlockSpec((1, MIX_WIDTH), const2),
                  pl.BlockSpec((MIX_WIDTH, D_MODEL), const2),
                  pl.BlockSpec((1, D_MODEL), const2)],
        out_specs=[pl.BlockSpec((tb, D_MODEL), row),
                   pl.BlockSpec((D_MODEL, tb), lambda i: (0, i))],
        out_shape=[jax.ShapeDtypeStruct((T, D_MODEL), f32),
                   jax.ShapeDtypeStruct((D_MODEL, T), bf16)],
        compiler_params=_cparams(("parallel",)),
        name="mix_out",
    )(oatt_t, gm, cv, cv, x2, sgug, wsp, bspf, cw, mg, wout, n2g)


def _colmax(x):
    return jnp.max(x, axis=0, keepdims=True)


def _top16(s, vals_ref):
    rank = jnp.full(s.shape, float(PEER_TOPK), f32)
    for a in range(PEER_TOPK):
        m = _colmax(s)
        hit = s == m
        rank = jnp.where(hit, float(a), rank)
        s = jnp.where(hit, -jnp.inf, s)
        vals_ref[a:a + 1, :] = m
    return rank


def _peerprep_kernel(h2_ref, wq_ref, keys_ref, r2_ref, e2_ref, n1_ref, c1_ref,
                     v1_ref, v2_ref):
    qT = jnp.dot(wq_ref[...], h2_ref[...], preferred_element_type=f32)
    qb = qT.astype(bf16)
    s1 = jnp.dot(keys_ref[0], qb[:D_HALF], preferred_element_type=f32)
    s2 = jnp.dot(keys_ref[1], qb[D_HALF:], preferred_element_type=f32)
    r1 = _top16(s1, v1_ref)
    r2 = _top16(s2, v2_ref)
    v1 = v1_ref[...]
    v2 = v2_ref[...]
    cands = [v1[0:1] + v2]
    cands += [v1[a:a + 1] + v2[0:8] for a in range(1, 8)]
    cands += [v1[8:16] + v2[0:1]]
    cmax = v1[0:1] + v2[0:1]
    work = list(cands)
    tau = cmax
    cnt = jnp.zeros_like(cmax)
    for _ in range(PEER_TOPK):
        m = _colmax(functools.reduce(jnp.maximum, work[1:], work[0][0:8]))
        m = jnp.maximum(m, _colmax(work[0][8:16]))
        tau = jnp.where(cnt < float(PEER_TOPK), m, tau)
        hits = [w == m for w in work]
        cnt = cnt + sum(jnp.sum(jnp.where(h, 1.0, 0.0), axis=0, keepdims=True) for h in hits)
        work = [jnp.where(h, -jnp.inf, w) for h, w in zip(hits, work)]
    sel = [c >= tau for c in cands]
    zsum = sum(jnp.sum(jnp.where(s_, jnp.exp(c - cmax), 0.0), axis=0, keepdims=True)
               for s_, c in zip(sel, cands))
    n_lo = [jnp.sum(jnp.where(s_, 1.0, 0.0), axis=0, keepdims=True) for s_ in sel[:8]]
    n_hi = jnp.where(sel[8], 1.0, 0.0)
    n1 = jnp.zeros(r1.shape, f32)
    for a in range(8):
        n1 = jnp.where(r1 == float(a), n_lo[a], n1)
    for a in range(8, PEER_TOPK):
        n1 = jnp.where(r1 == float(a), n_hi[a - 8:a - 7], n1)
    e2 = jnp.exp(s2 - v2[0:1])
    c1 = jnp.exp(s1 - v1[0:1]) / zsum
    for c in range(r2_ref.shape[1]):
        cs = slice(c * SLAB, (c + 1) * SLAB)
        r2_ref[0, c] = r2[:, cs].astype(bf16)
        e2_ref[0, c] = e2[:, cs].astype(bf16)
    for c in range(n1_ref.shape[1]):
        cs = slice(c * LANES, (c + 1) * LANES)
        n1_ref[0, c] = n1[:, cs]
        c1_ref[0, c] = c1[:, cs]


def _peer_prep(h2t, wqT, keys, tb):
    T = h2t.shape[1]
    spec = pl.BlockSpec((1, tb // SLAB, N_KEYS, SLAB), lambda t, h: (h, t, 0, 0))
    shp = (PEER_HEADS, T // SLAB, N_KEYS, SLAB)
    rspec = pl.BlockSpec((1, tb // LANES, N_KEYS, LANES), lambda t, h: (h, t, 0, 0))
    rshp = (PEER_HEADS, T // LANES, N_KEYS, LANES)
    return pl.pallas_call(
        _peerprep_kernel,
        grid=(T // tb, PEER_HEADS),
        in_specs=[pl.BlockSpec((D_MODEL, tb), lambda t, h: (0, t)),
                  pl.BlockSpec((D_KEY, D_MODEL), lambda t, h: (h, 0)),
                  pl.BlockSpec((2, N_KEYS, D_HALF), lambda t, h: (0, 0, 0))],
        out_specs=[spec, spec, rspec, rspec],
        out_shape=[jax.ShapeDtypeStruct(shp, bf16), jax.ShapeDtypeStruct(shp, bf16),
                   jax.ShapeDtypeStruct(rshp, f32), jax.ShapeDtypeStruct(rshp, f32)],
        scratch_shapes=[pltpu.VMEM((PEER_TOPK, tb), f32), pltpu.VMEM((PEER_TOPK, tb), f32)],
        compiler_params=_cparams(("parallel", "arbitrary")),
        name="peer_prep",
    )(h2t, wqT, keys)


def _peermain_kernel(h2_ref, u_ref, vt_ref, r2_ref, e2_ref, n1_ref, c1_ref, x_ref,
                     o_ref, acc_ref, w_ref, *, rows):
    e = pl.program_id(1)
    tb = h2_ref.shape[1]
    nc = tb // SLAB
    lpc = SLAB // LANES

    @pl.when(e == 0)
    def _():
        acc_ref[...] = jnp.zeros_like(acc_ref)

    base = e * rows

    def row_tile(ref, h, r, c):
        row = jnp.concatenate(
            [jnp.broadcast_to(ref[h, c * lpc + k, pl.ds(base + r, 1), :], (BF16_ROWS, LANES))
             for k in range(lpc)], axis=1).astype(bf16)
        return pltpu.repeat(row, N_KEYS // BF16_ROWS, axis=0)

    aT = jnp.dot(u_ref[...], h2_ref[...], preferred_element_type=f32)
    zero = jnp.zeros((N_KEYS, SLAB), bf16)
    for c in range(nc):
        cs = slice(c * SLAB, (c + 1) * SLAB)
        for r in range(rows):
            rs = slice(r * N_KEYS, (r + 1) * N_KEYS)
            g = None
            for h in range(PEER_HEADS):
                term = jnp.where(r2_ref[h, c] < row_tile(n1_ref, h, r, c), e2_ref[h, c], zero)
                term = term * row_tile(c1_ref, h, r, c)
                g = term if g is None else g + term
            w_ref[c, rs, :] = _gelu(aT[rs, cs]).astype(bf16) * g
    w = jnp.concatenate([w_ref[c] for c in range(nc)], axis=1)
    acc_ref[...] += jnp.dot(vt_ref[...], w, preferred_element_type=f32)

    @pl.when(e == pl.num_programs(1) - 1)
    def _():
        o_ref[...] = x_ref[...] + acc_ref[...].T


def _peer_main(h2t, u, vt, r2, e2, n1, c1, x2, tb, rows):
    T = h2t.shape[1]
    eb = rows * N_KEYS
    sel = pl.BlockSpec((PEER_HEADS, tb // SLAB, N_KEYS, SLAB), lambda t, e: (0, t, 0, 0))
    rsel = pl.BlockSpec((PEER_HEADS, tb // LANES, N_KEYS, LANES), lambda t, e: (0, t, 0, 0))
    return pl.pallas_call(
        functools.partial(_peermain_kernel, rows=rows),
        grid=(T // tb, N_KEYS // rows),
        in_specs=[pl.BlockSpec((D_MODEL, tb), lambda t, e: (0, t)),
                  pl.BlockSpec((eb, D_MODEL), lambda t, e: (e, 0)),
                  pl.BlockSpec((D_MODEL, eb), lambda t, e: (0, e)),
                  sel, sel, rsel, rsel,
                  pl.BlockSpec((tb, D_MODEL), lambda t, e: (t, 0))],
        out_specs=pl.BlockSpec((tb, D_MODEL), lambda t, e: (t, 0)),
        out_shape=jax.ShapeDtypeStruct((T, D_MODEL), f32),
        scratch_shapes=[pltpu.VMEM((D_MODEL, tb), f32),
                        pltpu.VMEM((tb // SLAB, eb, SLAB), bf16)],
        compiler_params=_cparams(("parallel", "arbitrary")),
        name="peer_main",
    )(h2t, u, vt, r2, e2, n1, c1, x2)


def _finalnorm_kernel(x_ref, g_ref, o_ref):
    o_ref[...] = _rms(x_ref[...], g_ref[...])


def _final_norm(x2, g, tb):
    T = x2.shape[0]
    return pl.pallas_call(
        _finalnorm_kernel,
        grid=(T // tb,),
        in_specs=[pl.BlockSpec((tb, D_MODEL), lambda i: (i, 0)),
                  pl.BlockSpec((1, D_MODEL), lambda i: (0, 0))],
        out_specs=pl.BlockSpec((tb, D_MODEL), lambda i: (i, 0)),
        out_shape=jax.ShapeDtypeStruct(x2.shape, f32),
        compiler_params=_cparams(("parallel",)),
        name="final_norm",
    )(x2, g)


def _block_sizes(S):
    return dict(tb_in=min(512, S), tc=min(512, S), tq=min(512, S), heads_per_step=4,
                tb_mix=min(256, S), tb_prep=min(256, S), tb_peer=min(512, S), rows=8)


def kernel(x, norm1_g, w_in, b_forget, sgu_norm_g, w_spatial, b_spatial, conv_w,
           mix_norm_g, w_out, norm2_g, w_query, sub_keys, expert_u, expert_v, final_g):
    B, S, D = x.shape
    T = B * S
    depth = w_in.shape[0]
    bs = _block_sizes(S)
    x2 = x.reshape(T, D)
    f0 = 3 * ATT_WIDTH
    for l in range(depth):
        wl = w_in[l]
        w_main = jnp.concatenate([wl[:, :f0], wl[:, f0 + ATT_HEADS:]], axis=1).astype(bf16)
        wf = jnp.pad(wl[:, f0:f0 + ATT_HEADS], ((0, 0), (0, LANES - ATT_HEADS)))
        bfp = jnp.pad(b_forget[l], (0, LANES - ATT_HEADS)).reshape(1, LANES)
        qkv, gm, cv, logf = _in_proj(x2, norm1_g[l].reshape(1, D), w_main, wf, bfp, bs["tb_in"])
        cum = _cumsum(logf, B, S, bs["tc"])
        qa, ka, vt = _attention_operands(qkv, cum, B, S)
        oatt_t = _attention(qa, ka, vt, bs["tq"], bs["heads_per_step"])
        bspf = jnp.repeat(b_spatial[l].T, GROUP_DIM, axis=1)
        cw = jnp.pad(conv_w[l], ((0, 8 - CONV_K), (0, 0)))
        x2, h2t = _mix_out(oatt_t, gm, cv, x2, sgu_norm_g[l].reshape(1, GMLP_WIDTH),
                          w_spatial[l], bspf, cw, mix_norm_g[l].reshape(1, MIX_WIDTH),
                          w_out[l].astype(bf16), norm2_g[l].reshape(1, D), S, bs["tb_mix"])
        r2, e2, n1, c1 = _peer_prep(h2t, w_query[l].T.astype(bf16), sub_keys[l].astype(bf16),
                                    bs["tb_prep"])
        x2 = _peer_main(h2t, expert_u[l].astype(bf16), expert_v[l].T.astype(bf16),
                        r2, e2, n1, c1, x2, bs["tb_peer"], bs["rows"])
    return _final_norm(x2, final_g.reshape(1, D), bs["tb_in"]).reshape(B, S, D)
```

```python
import functools
import math

import jax
import jax.numpy as jnp
from jax import lax
from jax.experimental import pallas as pl
from jax.experimental.pallas import tpu as pltpu

f32 = jnp.float32
bf16 = jnp.bfloat16

D_MODEL = 1024
ATT_WIDTH = 512
ATT_HEADS = 8
HEAD_DIM = 64
GMLP_WIDTH = 256
GMLP_GROUPS = 4
GROUP_DIM = 64
CHUNK = 128
CONV_WIDTH = 256
CONV_K = 3
MIX_WIDTH = 1024
PEER_HEADS = 8
N_KEYS = 128
PEER_TOPK = 16
D_KEY = 256
D_HALF = 128
EPS = 1e-6
LANES = 128
BF16_ROWS = 16
SLAB = 256
NEG_BIG = -1e30
LOG2E = math.log2(math.e)
F_PARTS = 3
V_ROWS = 80
UNDERFLOW_LOG2 = 160.0
NORM_SLACK = 1.02

VMEM_LIMIT = 56 * 1024 * 1024

_NT = (((1,), (1,)), ((), ()))


def _cparams(sem, flags=None):
    return pltpu.CompilerParams(dimension_semantics=sem, vmem_limit_bytes=VMEM_LIMIT, flags=flags)


def _gelu(x):
    z2 = (2.0 * 0.7978845608028654) * (x + 0.044715 * (x * x * x))
    return x / (1.0 + jnp.exp(-z2))


def _rms(x, g):
    return x * lax.rsqrt(jnp.mean(x * x, axis=-1, keepdims=True) + EPS) * g


def _split3(x):
    hi = x.astype(bf16)
    r = x - hi.astype(f32)
    mid = r.astype(bf16)
    lo = (r - mid.astype(f32)).astype(bf16)
    return hi, mid, lo


def _inproj_kernel(x_ref, g_ref, w_ref, wf_ref, bf_ref,
                   qa_ref, kp_ref, vt_ref, gm_ref, cv_ref, logf_ref, nrm_ref):
    tb = x_ref.shape[0]
    h = _rms(x_ref[...], g_ref[...])
    hb = h.astype(bf16)
    z = jnp.dot(hb, w_ref[...], preferred_element_type=f32)
    hw = ATT_HEADS * LANES
    lane = lax.broadcasted_iota(jnp.int32, (1, hw), 1) % LANES
    fcol = (lane >= HEAD_DIM) & (lane < HEAD_DIM + F_PARTS)
    zq = z[:, :hw] * (HEAD_DIM ** -0.5 * LOG2E)
    zk = z[:, hw:2 * hw]
    qa = jnp.where(fcol, -1.0, zq).astype(bf16)
    kp = zk.astype(bf16)
    vT = z[:, 2 * hw:2 * hw + ATT_WIDTH].T
    tail = jnp.where(lax.broadcasted_iota(jnp.int32, (V_ROWS - HEAD_DIM, tb), 0) == 0,
                     1.0, 0.0).astype(bf16)
    for hd in range(ATT_HEADS):
        qa_ref[0, hd] = qa[:, hd * LANES:(hd + 1) * LANES]
        kp_ref[0, hd] = kp[:, hd * LANES:(hd + 1) * LANES]
        vt_ref[0, hd, :HEAD_DIM, :] = vT[hd * HEAD_DIM:(hd + 1) * HEAD_DIM, :].astype(bf16)
        vt_ref[0, hd, HEAD_DIM:, :] = tail
    c0 = 2 * hw + ATT_WIDTH
    gm_ref[...] = z[:, c0:c0 + 2 * GMLP_WIDTH]
    cv_ref[...] = z[:, c0 + 2 * GMLP_WIDTH:]
    r = lax.broadcasted_iota(jnp.int32, (hw, LANES), 0) // LANES
    c = lax.broadcasted_iota(jnp.int32, (hw, LANES), 1)
    ind_q = jnp.where(r == c, 1.0, 0.0).astype(bf16)
    ind_k = jnp.where(r + ATT_HEADS == c, 1.0, 0.0).astype(bf16)
    nrm_ref[...] = (jnp.dot((zq * zq).astype(bf16), ind_q, preferred_element_type=f32)
                    + jnp.dot((zk * zk).astype(bf16), ind_k, preferred_element_type=f32))
    hl = (h - hb.astype(f32)).astype(bf16)
    wf = wf_ref[...]
    wfh = wf.astype(bf16)
    wfl = (wf - wfh.astype(f32)).astype(bf16)
    zf = (jnp.dot(hb, wfh, preferred_element_type=f32)
          + jnp.dot(hb, wfl, preferred_element_type=f32)
          + jnp.dot(hl, wfh, preferred_element_type=f32)) + bf_ref[...]
    logf_ref[...] = jnp.minimum(zf, 0.0) - jnp.log1p(jnp.exp(-jnp.abs(zf)))


def _in_proj(x2, g, w, wf, bfp, B, S, tb):
    T = x2.shape[0]
    bps = S // tb
    ncol = w.shape[1]
    const = lambda i: (0, 0)
    row = lambda i: (i, 0)
    head_spec = pl.BlockSpec((1, ATT_HEADS, tb, LANES), lambda i: (i // bps, 0, i % bps, 0))
    head_shape = jax.ShapeDtypeStruct((B, ATT_HEADS, S, LANES), bf16)
    return pl.pallas_call(
        _inproj_kernel,
        grid=(T // tb,),
        in_specs=[pl.BlockSpec((tb, D_MODEL), row),
                  pl.BlockSpec((1, D_MODEL), const),
                  pl.BlockSpec((D_MODEL, ncol), const),
                  pl.BlockSpec((D_MODEL, LANES), const),
                  pl.BlockSpec((1, LANES), const)],
        out_specs=[head_spec, head_spec,
                   pl.BlockSpec((1, ATT_HEADS, V_ROWS, tb), lambda i: (i // bps, 0, 0, i % bps)),
                   pl.BlockSpec((tb, 2 * GMLP_WIDTH), row),
                   pl.BlockSpec((tb, 3 * CONV_WIDTH), row),
                   pl.BlockSpec((tb, LANES), row),
                   pl.BlockSpec((tb, LANES), row)],
        out_shape=[head_shape, head_shape,
                   jax.ShapeDtypeStruct((B, ATT_HEADS, V_ROWS, S), bf16),
                   jax.ShapeDtypeStruct((T, 2 * GMLP_WIDTH), f32),
                   jax.ShapeDtypeStruct((T, 3 * CONV_WIDTH), f32),
                   jax.ShapeDtypeStruct((T, LANES), f32),
                   jax.ShapeDtypeStruct((T, LANES), f32)],
        compiler_params=_cparams(("parallel",)),
        name="in_proj",
    )(x2, g, w, wf, bfp)


def _in_proj_weights(wl):
    f0 = 3 * ATT_WIDTH

    def per_head(cols):
        w3 = cols.reshape(D_MODEL, ATT_HEADS, HEAD_DIM)
        return jnp.pad(w3, ((0, 0), (0, 0), (0, LANES - HEAD_DIM))).reshape(D_MODEL, -1)

    w_main = jnp.concatenate([per_head(wl[:, :ATT_WIDTH]), per_head(wl[:, ATT_WIDTH:2 * ATT_WIDTH]),
                              wl[:, 2 * ATT_WIDTH:f0], wl[:, f0 + ATT_HEADS:]], axis=1)
    wf = jnp.pad(wl[:, f0:f0 + ATT_HEADS], ((0, 0), (0, LANES - ATT_HEADS)))
    return w_main.astype(bf16), wf


def _cumsum_kernel(x_ref, kp_ref, ka_ref, carry_ref, *, tc):
    @pl.when(pl.program_id(1) == 0)
    def _():
        carry_ref[...] = jnp.zeros_like(carry_ref)

    row = lax.broadcasted_iota(jnp.int32, (tc, tc), 0)
    col = lax.broadcasted_iota(jnp.int32, (tc, tc), 1)
    tri = jnp.where(col <= row, 1.0, 0.0).astype(bf16)
    hi, mid, lo = _split3(x_ref[...])
    c = (jnp.dot(tri, hi, preferred_element_type=f32)
         + jnp.dot(tri, mid, preferred_element_type=f32)
         + jnp.dot(tri, lo, preferred_element_type=f32)) + carry_ref[...]
    carry_ref[...] = c[tc - 1:tc, :]
    hw = ATT_HEADS * LANES
    src = lax.broadcasted_iota(jnp.int32, (LANES, hw), 0)
    dst = lax.broadcasted_iota(jnp.int32, (LANES, hw), 1)
    fcols = jnp.zeros((tc, hw), f32)
    for p, part in enumerate(_split3(c * LOG2E)):
        place = jnp.where(dst == src * LANES + HEAD_DIM + p, 1.0, 0.0).astype(bf16)
        fcols = fcols + jnp.dot(part, place, preferred_element_type=f32)
    for hd in range(ATT_HEADS):
        ka_ref[0, hd] = (kp_ref[0, hd].astype(f32)
                         + fcols[:, hd * LANES:(hd + 1) * LANES]).astype(bf16)


def _cumsum(logf, kp, tc):
    B, H, S, _ = kp.shape
    nb = S // tc
    head_spec = pl.BlockSpec((1, H, tc, LANES), lambda b, j: (b, 0, j, 0))
    return pl.pallas_call(
        functools.partial(_cumsum_kernel, tc=tc),
        grid=(B, nb),
        in_specs=[pl.BlockSpec((tc, LANES), lambda b, j: (b * nb + j, 0)), head_spec],
        out_specs=head_spec,
        out_shape=jax.ShapeDtypeStruct(kp.shape, bf16),
        scratch_shapes=[pltpu.VMEM((1, LANES), f32)],
        compiler_params=_cparams(("arbitrary", "arbitrary")),
        name="cumsum_logf",
    )(logf, kp)


def _attn_kernel(jstart_ref, qa_ref, ka_ref, vt_ref, o_ref, acc_ref, m_ref, *, tq, hg):
    i = pl.program_id(2)
    m_ref[...] = jnp.full(m_ref.shape, NEG_BIG, f32)
    acc_ref[...] = jnp.zeros(acc_ref.shape, f32)

    def step(j, masked):
        start = pl.multiple_of(j * tq, tq)
        for h in range(hg):
            k = ka_ref[0, h, pl.ds(start, tq), :]
            s = lax.dot_general(k, qa_ref[0, h], _NT, preferred_element_type=f32)
            if masked:
                r = lax.broadcasted_iota(jnp.int32, (tq, tq), 0)
                c = lax.broadcasted_iota(jnp.int32, (tq, tq), 1)
                s = jnp.where(r <= c, s, NEG_BIG)
            m = m_ref[h]
            m_new = jnp.maximum(m, jnp.max(s, axis=0, keepdims=True))
            p = jnp.exp2(s - m_new).astype(bf16)
            pv = jnp.dot(vt_ref[0, h, :, pl.ds(start, tq)], p, preferred_element_type=f32)
            acc_ref[h] = jnp.exp2(m - m_new) * acc_ref[h] + pv
            m_ref[h] = m_new

    def body(j, carry):
        step(j, False)
        return carry

    lax.fori_loop(jstart_ref[pl.program_id(0), pl.program_id(1), i], i, body, 0)
    step(i, True)
    for h in range(hg):
        o_ref[0, h * HEAD_DIM:(h + 1) * HEAD_DIM, :] = (
            acc_ref[h, :HEAD_DIM, :] / acc_ref[h, HEAD_DIM:HEAD_DIM + 1, :])


def _attention(jstart, qa, ka, vt, tq, hg):
    B, H, S, _ = qa.shape
    return pl.pallas_call(
        functools.partial(_attn_kernel, tq=tq, hg=hg),
        grid_spec=pltpu.PrefetchScalarGridSpec(
            num_scalar_prefetch=1,
            grid=(B, H // hg, S // tq),
            in_specs=[pl.BlockSpec((1, hg, tq, LANES), lambda b, g, i, js: (b, g, i, 0)),
                      pl.BlockSpec((1, hg, S, LANES), lambda b, g, i, js: (b, g, 0, 0)),
                      pl.BlockSpec((1, hg, V_ROWS, S), lambda b, g, i, js: (b, g, 0, 0))],
            out_specs=pl.BlockSpec((1, hg * HEAD_DIM, tq), lambda b, g, i, js: (b, g, i)),
            scratch_shapes=[pltpu.VMEM((hg, V_ROWS, tq), f32), pltpu.VMEM((hg, 1, tq), f32)]),
        out_shape=jax.ShapeDtypeStruct((B, H * HEAD_DIM, S), f32),
        compiler_params=_cparams(("parallel", "parallel", "arbitrary")),
        name="fox_attention",
    )(jstart, qa, ka, vt)


def _first_key_block(nrm, ka, tq, hg):
    B, H, S, _ = ka.shape
    nq = S // tq
    n2 = jnp.max(nrm.reshape(B, nq, tq, LANES), axis=2)
    qmax = jnp.sqrt(n2[:, :, :H]).transpose(0, 2, 1) * NORM_SLACK
    kmax = jnp.sqrt(n2[:, :, H:2 * H]).transpose(0, 2, 1) * NORM_SLACK
    kcum = lax.cummax(kmax, axis=2)
    fl = ka[:, :, :, HEAD_DIM:HEAD_DIM + F_PARTS].astype(f32).sum(-1)
    f_end = fl[:, :, tq - 1::tq]
    f_start = fl[:, :, ::tq]
    upper = qmax[:, :, :, None] * kcum[:, :, None, :] - f_end[:, :, None, :]
    lower = -qmax * kmax - f_start
    skip = upper < lower[:, :, :, None] - UNDERFLOW_LOG2
    lead = jnp.cumprod(skip.astype(jnp.int32), axis=3).sum(axis=3)
    lead = jnp.minimum(lead, jnp.arange(nq, dtype=jnp.int32)[None, None, :])
    return lead.reshape(B, H // hg, hg, nq).min(axis=2).astype(jnp.int32)


def _mixout_kernel(oatt_ref, gm_ref, cv_ref, cvp_ref, x_ref, sgug_ref, wsp_ref, bsp_ref,
                   cw_ref, mg_ref, wout_ref, n2g_ref, xo_ref, h2_ref, *, tb, blocks_per_seq):
    i = pl.program_id(0)
    gm = gm_ref[...]
    u = _gelu(gm[:, :GMLP_WIDTH])
    v = _gelu(gm[:, GMLP_WIDTH:])
    gr = lax.broadcasted_iota(jnp.int32, (GMLP_WIDTH, GMLP_WIDTH), 0) // GROUP_DIM
    gc = lax.broadcasted_iota(jnp.int32, (GMLP_WIDTH, GMLP_WIDTH), 1) // GROUP_DIM
    grp = jnp.where(gr == gc, 1.0 / GROUP_DIM, 0.0).astype(bf16)
    vh, vm, vl = _split3(v * v)
    ms = (jnp.dot(vh, grp, preferred_element_type=f32)
          + jnp.dot(vm, grp, preferred_element_type=f32)
          + jnp.dot(vl, grp, preferred_element_type=f32))
    vn = (v * lax.rsqrt(ms + EPS) * sgug_ref[...]).astype(bf16)
    tr = lax.broadcasted_iota(jnp.int32, (CHUNK, CHUNK), 0)
    tc = lax.broadcasted_iota(jnp.int32, (CHUNK, CHUNK), 1)
    lane_g = lax.broadcasted_iota(jnp.int32, (CHUNK, GMLP_WIDTH), 1) // GROUP_DIM
    mixed_chunks = []
    for c in range(tb // CHUNK):
        vc = vn[c * CHUNK:(c + 1) * CHUNK, :]
        mixed = jnp.zeros((CHUNK, GMLP_WIDTH), f32)
        for g in range(GMLP_GROUPS):
            wg = jnp.where(tc <= tr, wsp_ref[g], 0.0).astype(bf16)
            mg = jnp.dot(wg, vc, preferred_element_type=f32)
            mixed = jnp.where(lane_g == g, mg, mixed)
        mixed_chunks.append(mixed + bsp_ref[...])
    o_gmlp = u * jnp.concatenate(mixed_chunks, axis=0)
    cv = cv_ref[...]
    c_in = cv[:, :CONV_WIDTH]
    c_b = cv[:, CONV_WIDTH:2 * CONV_WIDTH]
    c_c = cv[:, 2 * CONV_WIDTH:]
    z = c_c * c_in
    cvp = cvp_ref[...]
    zp = cvp[:, 2 * CONV_WIDTH:] * cvp[:, :CONV_WIDTH]
    zp = jnp.where(i % blocks_per_seq == 0, jnp.zeros_like(zp), zp)
    rowi = lax.broadcasted_iota(jnp.int32, (tb, CONV_WIDTH), 0)
    z1 = pltpu.roll(z, 1, axis=0)
    z1 = jnp.where(rowi == 0, zp[7:8, :], z1)
    z2 = pltpu.roll(z, 2, axis=0)
    z2 = jnp.where(rowi == 0, zp[6:7, :], z2)
    z2 = jnp.where(rowi == 1, zp[7:8, :], z2)
    cw = cw_ref[...]
    o_conv = c_b * (cw[0:1, :] * z2 + cw[1:2, :] * z1 + cw[2:3, :] * z)
    mg_ = mg_ref[...]
    o_att = oatt_ref[0].T
    mixed_all = jnp.concatenate([
        _rms(o_att, mg_[:, :ATT_WIDTH]),
        _rms(o_gmlp, mg_[:, ATT_WIDTH:ATT_WIDTH + GMLP_WIDTH]),
        _rms(o_conv, mg_[:, ATT_WIDTH + GMLP_WIDTH:])], axis=-1).astype(bf16)
    xn = x_ref[...] + jnp.dot(mixed_all, wout_ref[...], preferred_element_type=f32)
    xo_ref[...] = xn
    h2_ref[...] = _rms(xn, n2g_ref[...]).T.astype(bf16)


def _mix_out(oatt_t, gm, cv, x2, sgug, wsp, bspf, cw, mg, wout, n2g, S, tb):
    T = x2.shape[0]
    bps = S // tb
    const2 = lambda i: (0, 0)
    row = lambda i: (i, 0)
    return pl.pallas_call(
        functools.partial(_mixout_kernel, tb=tb, blocks_per_seq=bps),
        grid=(T // tb,),
        in_specs=[pl.BlockSpec((1, ATT_WIDTH, tb), lambda i: (i // bps, 0, i % bps)),
                  pl.BlockSpec((tb, 2 * GMLP_WIDTH), row),
                  pl.BlockSpec((tb, 3 * CONV_WIDTH), row),
                  pl.BlockSpec((8, 3 * CONV_WIDTH),
                               lambda i: (jnp.maximum(i * (tb // 8) - 1, 0), 0)),
                  pl.BlockSpec((tb, D_MODEL), row),
                  pl.BlockSpec((1, GMLP_WIDTH), const2),
                  pl.BlockSpec((GMLP_GROUPS, CHUNK, CHUNK), lambda i: (0, 0, 0)),
                  pl.BlockSpec((CHUNK, GMLP_WIDTH), const2),
                  pl.BlockSpec((8, CONV_WIDTH), const2),
                  pl.BlockSpec((1, MIX_WIDTH), const2),
                  pl.BlockSpec((MIX_WIDTH, D_MODEL), const2),
                  pl.BlockSpec((1, D_MODEL), const2)],
        out_specs=[pl.BlockSpec((tb, D_MODEL), row),
                   pl.BlockSpec((D_MODEL, tb), lambda i: (0, i))],
        out_shape=[jax.ShapeDtypeStruct((T, D_MODEL), f32),
                   jax.ShapeDtypeStruct((D_MODEL, T), bf16)],
        compiler_params=_cparams(("parallel",)),
        name="mix_out",
    )(oatt_t, gm, cv, cv, x2, sgug, wsp, bspf, cw, mg, wout, n2g)


def _colmax(x):
    return jnp.max(x, axis=0, keepdims=True)


def _top16(s, vals_ref):
    rank = jnp.full(s.shape, float(PEER_TOPK), f32)
    for a in range(PEER_TOPK):
        m = _colmax(s)
        hit = s == m
        rank = jnp.where(hit, float(a), rank)
        s = jnp.where(hit, -jnp.inf, s)
        vals_ref[a:a + 1, :] = m
    return rank


def _select_tables(s1, s2, v1_ref, v2_ref):
    r1 = _top16(s1, v1_ref)
    r2 = _top16(s2, v2_ref)
    v1 = v1_ref[...]
    v2 = v2_ref[...]
    cands = [v1[0:1] + v2]
    cands += [v1[a:a + 1] + v2[0:8] for a in range(1, 8)]
    cands += [v1[8:16] + v2[0:1]]
    cmax = v1[0:1] + v2[0:1]
    work = list(cands)
    tau = cmax
    cnt = jnp.zeros_like(cmax)
    for _ in range(PEER_TOPK):
        m = _colmax(functools.reduce(jnp.maximum, work[1:], work[0][0:8]))
        m = jnp.maximum(m, _colmax(work[0][8:16]))
        tau = jnp.where(cnt < float(PEER_TOPK), m, tau)
        hits = [w == m for w in work]
        cnt = cnt + sum(jnp.sum(jnp.where(h, 1.0, 0.0), axis=0, keepdims=True) for h in hits)
        work = [jnp.where(h, -jnp.inf, w) for h, w in zip(hits, work)]
    sel = [c >= tau for c in cands]
    zsum = sum(jnp.sum(jnp.where(s_, jnp.exp(c - cmax), 0.0), axis=0, keepdims=True)
               for s_, c in zip(sel, cands))
    n_lo = [jnp.sum(jnp.where(s_, 1.0, 0.0), axis=0, keepdims=True) for s_ in sel[:8]]
    n_hi = jnp.where(sel[8], 1.0, 0.0)
    n1 = jnp.zeros(r1.shape, f32)
    for a in range(8):
        n1 = jnp.where(r1 == float(a), n_lo[a], n1)
    for a in range(8, PEER_TOPK):
        n1 = jnp.where(r1 == float(a), n_hi[a - 8:a - 7], n1)
    e2 = jnp.exp(s2 - v2[0:1])
    c1 = jnp.exp(s1 - v1[0:1]) / zsum
    return r2.astype(bf16), e2.astype(bf16), n1, c1


def _peerprep_kernel(h2_ref, wq_ref, keys_ref, r2_ref, e2_ref, n1_ref, c1_ref,
                     v1_ref, v2_ref):
    qT = jnp.dot(wq_ref[...], h2_ref[...], preferred_element_type=f32)
    qb = qT.astype(bf16)
    s1 = jnp.dot(keys_ref[0], qb[:D_HALF], preferred_element_type=f32)
    s2 = jnp.dot(keys_ref[1], qb[D_HALF:], preferred_element_type=f32)
    lpc = SLAB // LANES
    for c in range(r2_ref.shape[1]):
        r2s, e2s = [], []
        for k in range(lpc):
            t = c * lpc + k
            ls = slice(t * LANES, (t + 1) * LANES)
            r2, e2, n1, c1 = _select_tables(s1[:, ls], s2[:, ls], v1_ref, v2_ref)
            r2s.append(r2)
            e2s.append(e2)
            n1_ref[0, t] = n1
            c1_ref[0, t] = c1
        r2_ref[0, c] = jnp.concatenate(r2s, axis=1)
        e2_ref[0, c] = jnp.concatenate(e2s, axis=1)


def _peer_prep(h2t, wqT, keys, tb):
    T = h2t.shape[1]
    spec = pl.BlockSpec((1, tb // SLAB, N_KEYS, SLAB), lambda t, h: (h, t, 0, 0))
    shp = (PEER_HEADS, T // SLAB, N_KEYS, SLAB)
    rspec = pl.BlockSpec((1, tb // LANES, N_KEYS, LANES), lambda t, h: (h, t, 0, 0))
    rshp = (PEER_HEADS, T // LANES, N_KEYS, LANES)
    return pl.pallas_call(
        _peerprep_kernel,
        grid=(T // tb, PEER_HEADS),
        in_specs=[pl.BlockSpec((D_MODEL, tb), lambda t, h: (0, t)),
                  pl.BlockSpec((D_KEY, D_MODEL), lambda t, h: (h, 0)),
                  pl.BlockSpec((2, N_KEYS, D_HALF), lambda t, h: (0, 0, 0))],
        out_specs=[spec, spec, rspec, rspec],
        out_shape=[jax.ShapeDtypeStruct(shp, bf16), jax.ShapeDtypeStruct(shp, bf16),
                   jax.ShapeDtypeStruct(rshp, f32), jax.ShapeDtypeStruct(rshp, f32)],
        scratch_shapes=[pltpu.VMEM((PEER_TOPK, LANES), f32), pltpu.VMEM((PEER_TOPK, LANES), f32)],
        compiler_params=_cparams(("parallel", "arbitrary")),
        name="peer_prep",
    )(h2t, wqT, keys)


def _peermain_kernel(h2_ref, u_ref, vt_ref, r2_ref, e2_ref, n1_ref, c1_ref, x_ref,
                     o_ref, acc_ref, w_ref, *, rows):
    e = pl.program_id(1)
    tb = h2_ref.shape[1]
    nc = tb // SLAB
    lpc = SLAB // LANES

    @pl.when(e == 0)
    def _():
        acc_ref[...] = jnp.zeros_like(acc_ref)

    base = e * rows

    def row_tile(ref, h, r, c):
        row = jnp.concatenate(
            [jnp.broadcast_to(ref[h, c * lpc + k, pl.ds(base + r, 1), :], (BF16_ROWS, LANES))
             for k in range(lpc)], axis=1).astype(bf16)
        return jnp.tile(row, (N_KEYS // BF16_ROWS, 1))

    aT = jnp.dot(u_ref[...], h2_ref[...], preferred_element_type=f32)
    zero = jnp.zeros((N_KEYS, SLAB), bf16)
    for c in range(nc):
        cs = slice(c * SLAB, (c + 1) * SLAB)
        for r in range(rows):
            rs = slice(r * N_KEYS, (r + 1) * N_KEYS)
            g = None
            for h in range(PEER_HEADS):
                term = jnp.where(r2_ref[h, c] < row_tile(n1_ref, h, r, c), e2_ref[h, c], zero)
                term = term * row_tile(c1_ref, h, r, c)
                g = term if g is None else g + term
            w_ref[c, rs, :] = _gelu(aT[rs, cs]).astype(bf16) * g
    w = jnp.concatenate([w_ref[c] for c in range(nc)], axis=1)
    acc_ref[...] += jnp.dot(vt_ref[...], w, preferred_element_type=f32)

    @pl.when(e == pl.num_programs(1) - 1)
    def _():
        o_ref[...] = x_ref[...] + acc_ref[...].T


def _peer_main(h2t, u, vt, r2, e2, n1, c1, x2, tb, rows):
    T = h2t.shape[1]
    eb = rows * N_KEYS
    sel = pl.BlockSpec((PEER_HEADS, tb // SLAB, N_KEYS, SLAB), lambda t, e: (0, t, 0, 0))
    rsel = pl.BlockSpec((PEER_HEADS, tb // LANES, N_KEYS, LANES), lambda t, e: (0, t, 0, 0))
    return pl.pallas_call(
        functools.partial(_peermain_kernel, rows=rows),
        grid=(T // tb, N_KEYS // rows),
        in_specs=[pl.BlockSpec((D_MODEL, tb), lambda t, e: (0, t)),
                  pl.BlockSpec((eb, D_MODEL), lambda t, e: (e, 0)),
                  pl.BlockSpec((D_MODEL, eb), lambda t, e: (0, e)),
                  sel, sel, rsel, rsel,
                  pl.BlockSpec((tb, D_MODEL), lambda t, e: (t, 0))],
        out_specs=pl.BlockSpec((tb, D_MODEL), lambda t, e: (t, 0)),
        out_shape=jax.ShapeDtypeStruct((T, D_MODEL), f32),
        scratch_shapes=[pltpu.VMEM((D_MODEL, tb), f32),
                        pltpu.VMEM((tb // SLAB, eb, SLAB), bf16)],
        compiler_params=_cparams(("parallel", "arbitrary")),
        name="peer_main",
    )(h2t, u, vt, r2, e2, n1, c1, x2)


def _finalnorm_kernel(x_ref, g_ref, o_ref):
    o_ref[...] = _rms(x_ref[...], g_ref[...])


def _final_norm(x2, g, tb):
    T = x2.shape[0]
    return pl.pallas_call(
        _finalnorm_kernel,
        grid=(T // tb,),
        in_specs=[pl.BlockSpec((tb, D_MODEL), lambda i: (i, 0)),
                  pl.BlockSpec((1, D_MODEL), lambda i: (0, 0))],
        out_specs=pl.BlockSpec((tb, D_MODEL), lambda i: (i, 0)),
        out_shape=jax.ShapeDtypeStruct(x2.shape, f32),
        compiler_params=_cparams(("parallel",)),
        name="final_norm",
    )(x2, g)


def _block_sizes(S):
    return dict(tb_in=min(512, S), tc=min(512, S), tq=min(512, S), heads_per_step=4,
                tb_mix=min(256, S), tb_prep=min(256, S), tb_peer=min(512, S), rows=8)


def kernel(x, norm1_g, w_in, b_forget, sgu_norm_g, w_spatial, b_spatial, conv_w,
           mix_norm_g, w_out, norm2_g, w_query, sub_keys, expert_u, expert_v, final_g):
    B, S, D = x.shape
    T = B * S
    depth = w_in.shape[0]
    bs = _block_sizes(S)
    x2 = x.reshape(T, D)
    for l in range(depth):
        w_main, wf = _in_proj_weights(w_in[l])
        bfp = jnp.pad(b_forget[l], (0, LANES - ATT_HEADS)).reshape(1, LANES)
        qa, kp, vt, gm, cv, logf, nrm = _in_proj(x2, norm1_g[l].reshape(1, D), w_main, wf, bfp,
                                                 B, S, bs["tb_in"])
        ka = _cumsum(logf, kp, bs["tc"])
        jstart = _first_key_block(nrm, ka, bs["tq"], bs["heads_per_step"])
        oatt_t = _attention(jstart, qa, ka, vt, bs["tq"], bs["heads_per_step"])
        bspf = jnp.repeat(b_spatial[l].T, GROUP_DIM, axis=1)
        cw = jnp.pad(conv_w[l], ((0, 8 - CONV_K), (0, 0)))
        x2, h2t = _mix_out(oatt_t, gm, cv, x2, sgu_norm_g[l].reshape(1, GMLP_WIDTH),
                          w_spatial[l], bspf, cw, mix_norm_g[l].reshape(1, MIX_WIDTH),
                          w_out[l].astype(bf16), norm2_g[l].reshape(1, D), S, bs["tb_mix"])
        r2, e2, n1, c1 = _peer_prep(h2t, w_query[l].T.astype(bf16), sub_keys[l].astype(bf16),
                                    bs["tb_prep"])
        x2 = _peer_main(h2t, expert_u[l].astype(bf16), expert_v[l].T.astype(bf16),
                        r2, e2, n1, c1, x2, bs["tb_peer"], bs["rows"])
    return _final_norm(x2, final_g.reshape(1, D), bs["tb_in"]).reshape(B, S, D)
```

```python
import functools
import math

import jax
import jax.numpy as jnp
from jax import lax
from jax.experimental import pallas as pl
from jax.experimental.pallas import tpu as pltpu

f32 = jnp.float32
bf16 = jnp.bfloat16

D_MODEL = 1024
ATT_WIDTH = 512
ATT_HEADS = 8
HEAD_DIM = 64
GMLP_WIDTH = 256
GMLP_GROUPS = 4
GROUP_DIM = 64
CHUNK = 128
CONV_WIDTH = 256
CONV_K = 3
MIX_WIDTH = 1024
PEER_HEADS = 8
N_KEYS = 128
PEER_TOPK = 16
D_KEY = 256
D_HALF = 128
EPS = 1e-6
LANES = 128
BF16_ROWS = 16
SLAB = 256
NEG_BIG = -1e30
LOG2E = math.log2(math.e)
F_PARTS = 3
V_ROWS = 80
UNDERFLOW_LOG2 = 160.0
NORM_SLACK = 1.02
RANK_MARK = -3.0e38
RANK_STEP = 2.0 ** -10

VMEM_LIMIT = 56 * 1024 * 1024

_NT = (((1,), (1,)), ((), ()))


def _cparams(sem, flags=None):
    return pltpu.CompilerParams(dimension_semantics=sem, vmem_limit_bytes=VMEM_LIMIT, flags=flags)


def _gelu(x):
    z2 = (2.0 * 0.7978845608028654) * (x + 0.044715 * (x * x * x))
    return x / (1.0 + jnp.exp(-z2))


def _rms(x, g):
    return x * lax.rsqrt(jnp.mean(x * x, axis=-1, keepdims=True) + EPS) * g


def _split3(x):
    hi = x.astype(bf16)
    r = x - hi.astype(f32)
    mid = r.astype(bf16)
    lo = (r - mid.astype(f32)).astype(bf16)
    return hi, mid, lo


def _inproj_kernel(x_ref, g_ref, w_ref, wf_ref, bf_ref,
                   qa_ref, kp_ref, vt_ref, gm_ref, cv_ref, logf_ref, nrm_ref):
    tb = x_ref.shape[0]
    h = _rms(x_ref[...], g_ref[...])
    hb = h.astype(bf16)
    z = jnp.dot(hb, w_ref[...], preferred_element_type=f32)
    hw = ATT_HEADS * LANES
    lane = lax.broadcasted_iota(jnp.int32, (1, hw), 1) % LANES
    fcol = (lane >= HEAD_DIM) & (lane < HEAD_DIM + F_PARTS)
    zq = z[:, :hw] * (HEAD_DIM ** -0.5 * LOG2E)
    zk = z[:, hw:2 * hw]
    qa = jnp.where(fcol, -1.0, zq).astype(bf16)
    kp = zk.astype(bf16)
    vT = z[:, 2 * hw:2 * hw + ATT_WIDTH].T
    tail = jnp.where(lax.broadcasted_iota(jnp.int32, (V_ROWS - HEAD_DIM, tb), 0) == 0,
                     1.0, 0.0).astype(bf16)
    for hd in range(ATT_HEADS):
        qa_ref[0, hd] = qa[:, hd * LANES:(hd + 1) * LANES]
        kp_ref[0, hd] = kp[:, hd * LANES:(hd + 1) * LANES]
        vt_ref[0, hd, :HEAD_DIM, :] = vT[hd * HEAD_DIM:(hd + 1) * HEAD_DIM, :].astype(bf16)
        vt_ref[0, hd, HEAD_DIM:, :] = tail
    c0 = 2 * hw + ATT_WIDTH
    gm_ref[...] = z[:, c0:c0 + 2 * GMLP_WIDTH]
    cv_ref[...] = z[:, c0 + 2 * GMLP_WIDTH:]
    r = lax.broadcasted_iota(jnp.int32, (hw, LANES), 0) // LANES
    c = lax.broadcasted_iota(jnp.int32, (hw, LANES), 1)
    ind_q = jnp.where(r == c, 1.0, 0.0).astype(bf16)
    ind_k = jnp.where(r + ATT_HEADS == c, 1.0, 0.0).astype(bf16)
    nrm_ref[...] = (jnp.dot((zq * zq).astype(bf16), ind_q, preferred_element_type=f32)
                    + jnp.dot((zk * zk).astype(bf16), ind_k, preferred_element_type=f32))
    hl = (h - hb.astype(f32)).astype(bf16)
    wf = wf_ref[...]
    wfh = wf.astype(bf16)
    wfl = (wf - wfh.astype(f32)).astype(bf16)
    zf = (jnp.dot(hb, wfh, preferred_element_type=f32)
          + jnp.dot(hb, wfl, preferred_element_type=f32)
          + jnp.dot(hl, wfh, preferred_element_type=f32)) + bf_ref[...]
    logf_ref[...] = jnp.minimum(zf, 0.0) - jnp.log1p(jnp.exp(-jnp.abs(zf)))


def _in_proj(x2, g, w, wf, bfp, B, S, tb):
    T = x2.shape[0]
    bps = S // tb
    ncol = w.shape[1]
    const = lambda i: (0, 0)
    row = lambda i: (i, 0)
    head_spec = pl.BlockSpec((1, ATT_HEADS, tb, LANES), lambda i: (i // bps, 0, i % bps, 0))
    head_shape = jax.ShapeDtypeStruct((B, ATT_HEADS, S, LANES), bf16)
    return pl.pallas_call(
        _inproj_kernel,
        grid=(T // tb,),
        in_specs=[pl.BlockSpec((tb, D_MODEL), row),
                  pl.BlockSpec((1, D_MODEL), const),
                  pl.BlockSpec((D_MODEL, ncol), const),
                  pl.BlockSpec((D_MODEL, LANES), const),
                  pl.BlockSpec((1, LANES), const)],
        out_specs=[head_spec, head_spec,
                   pl.BlockSpec((1, ATT_HEADS, V_ROWS, tb), lambda i: (i // bps, 0, 0, i % bps)),
                   pl.BlockSpec((tb, 2 * GMLP_WIDTH), row),
                   pl.BlockSpec((tb, 3 * CONV_WIDTH), row),
                   pl.BlockSpec((tb, LANES), row),
                   pl.BlockSpec((tb, LANES), row)],
        out_shape=[head_shape, head_shape,
                   jax.ShapeDtypeStruct((B, ATT_HEADS, V_ROWS, S), bf16),
                   jax.ShapeDtypeStruct((T, 2 * GMLP_WIDTH), f32),
                   jax.ShapeDtypeStruct((T, 3 * CONV_WIDTH), f32),
                   jax.ShapeDtypeStruct((T, LANES), f32),
                   jax.ShapeDtypeStruct((T, LANES), f32)],
        compiler_params=_cparams(("parallel",)),
        name="in_proj",
    )(x2, g, w, wf, bfp)


def _in_proj_weights(wl):
    f0 = 3 * ATT_WIDTH

    def per_head(cols):
        w3 = cols.reshape(D_MODEL, ATT_HEADS, HEAD_DIM)
        return jnp.pad(w3, ((0, 0), (0, 0), (0, LANES - HEAD_DIM))).reshape(D_MODEL, -1)

    w_main = jnp.concatenate([per_head(wl[:, :ATT_WIDTH]), per_head(wl[:, ATT_WIDTH:2 * ATT_WIDTH]),
                              wl[:, 2 * ATT_WIDTH:f0], wl[:, f0 + ATT_HEADS:]], axis=1)
    wf = jnp.pad(wl[:, f0:f0 + ATT_HEADS], ((0, 0), (0, LANES - ATT_HEADS)))
    return w_main.astype(bf16), wf


def _cumsum_kernel(x_ref, kp_ref, ka_ref, carry_ref, *, tc):
    @pl.when(pl.program_id(1) == 0)
    def _():
        carry_ref[...] = jnp.zeros_like(carry_ref)

    row = lax.broadcasted_iota(jnp.int32, (tc, tc), 0)
    col = lax.broadcasted_iota(jnp.int32, (tc, tc), 1)
    tri = jnp.where(col <= row, 1.0, 0.0).astype(bf16)
    hi, mid, lo = _split3(x_ref[...])
    c = (jnp.dot(tri, hi, preferred_element_type=f32)
         + jnp.dot(tri, mid, preferred_element_type=f32)
         + jnp.dot(tri, lo, preferred_element_type=f32)) + carry_ref[...]
    carry_ref[...] = c[tc - 1:tc, :]
    hw = ATT_HEADS * LANES
    src = lax.broadcasted_iota(jnp.int32, (LANES, hw), 0)
    dst = lax.broadcasted_iota(jnp.int32, (LANES, hw), 1)
    fcols = jnp.zeros((tc, hw), f32)
    for p, part in enumerate(_split3(c * LOG2E)):
        place = jnp.where(dst == src * LANES + HEAD_DIM + p, 1.0, 0.0).astype(bf16)
        fcols = fcols + jnp.dot(part, place, preferred_element_type=f32)
    for hd in range(ATT_HEADS):
        ka_ref[0, hd] = (kp_ref[0, hd].astype(f32)
                         + fcols[:, hd * LANES:(hd + 1) * LANES]).astype(bf16)


def _cumsum(logf, kp, tc):
    B, H, S, _ = kp.shape
    nb = S // tc
    head_spec = pl.BlockSpec((1, H, tc, LANES), lambda b, j: (b, 0, j, 0))
    return pl.pallas_call(
        functools.partial(_cumsum_kernel, tc=tc),
        grid=(B, nb),
        in_specs=[pl.BlockSpec((tc, LANES), lambda b, j: (b * nb + j, 0)), head_spec],
        out_specs=head_spec,
        out_shape=jax.ShapeDtypeStruct(kp.shape, bf16),
        scratch_shapes=[pltpu.VMEM((1, LANES), f32)],
        compiler_params=_cparams(("arbitrary", "arbitrary")),
        name="cumsum_logf",
    )(logf, kp)


def _attn_kernel(jstart_ref, qa_ref, ka_ref, vt_ref, o_ref, acc_ref, m_ref, *, tq, hg):
    i = pl.program_id(2)
    m_ref[...] = jnp.full(m_ref.shape, NEG_BIG, f32)
    acc_ref[...] = jnp.zeros(acc_ref.shape, f32)

    def step(j, masked):
        start = pl.multiple_of(j * tq, tq)
        for h in range(hg):
            k = ka_ref[0, h, pl.ds(start, tq), :]
            s = lax.dot_general(k, qa_ref[0, h], _NT, preferred_element_type=f32)
            if masked:
                r = lax.broadcasted_iota(jnp.int32, (tq, tq), 0)
                c = lax.broadcasted_iota(jnp.int32, (tq, tq), 1)
                s = jnp.where(r <= c, s, NEG_BIG)
            m = m_ref[h]
            m_new = jnp.maximum(m, jnp.max(s, axis=0, keepdims=True))
            p = jnp.exp2(s - m_new).astype(bf16)
            pv = jnp.dot(vt_ref[0, h, :, pl.ds(start, tq)], p, preferred_element_type=f32)
            acc_ref[h] = jnp.exp2(m - m_new) * acc_ref[h] + pv
            m_ref[h] = m_new

    def body(j, carry):
        step(j, False)
        return carry

    lax.fori_loop(jstart_ref[pl.program_id(0), pl.program_id(1), i], i, body, 0)
    step(i, True)
    for h in range(hg):
        o_ref[0, h * HEAD_DIM:(h + 1) * HEAD_DIM, :] = (
            acc_ref[h, :HEAD_DIM, :] / acc_ref[h, HEAD_DIM:HEAD_DIM + 1, :])


def _attention(jstart, qa, ka, vt, tq, hg):
    B, H, S, _ = qa.shape
    return pl.pallas_call(
        functools.partial(_attn_kernel, tq=tq, hg=hg),
        grid_spec=pltpu.PrefetchScalarGridSpec(
            num_scalar_prefetch=1,
            grid=(B, H // hg, S // tq),
            in_specs=[pl.BlockSpec((1, hg, tq, LANES), lambda b, g, i, js: (b, g, i, 0)),
                      pl.BlockSpec((1, hg, S, LANES), lambda b, g, i, js: (b, g, 0, 0)),
                      pl.BlockSpec((1, hg, V_ROWS, S), lambda b, g, i, js: (b, g, 0, 0))],
            out_specs=pl.BlockSpec((1, hg * HEAD_DIM, tq), lambda b, g, i, js: (b, g, i)),
            scratch_shapes=[pltpu.VMEM((hg, V_ROWS, tq), f32), pltpu.VMEM((hg, 1, tq), f32)]),
        out_shape=jax.ShapeDtypeStruct((B, H * HEAD_DIM, S), f32),
        compiler_params=_cparams(("parallel", "parallel", "arbitrary")),
        name="fox_attention",
    )(jstart, qa, ka, vt)


def _first_key_block(nrm, ka, tq, hg):
    B, H, S, _ = ka.shape
    nq = S // tq
    n2 = jnp.max(nrm.reshape(B, nq, tq, LANES), axis=2)
    qmax = jnp.sqrt(n2[:, :, :H]).transpose(0, 2, 1) * NORM_SLACK
    kmax = jnp.sqrt(n2[:, :, H:2 * H]).transpose(0, 2, 1) * NORM_SLACK
    blk = jnp.arange(nq, dtype=jnp.int32)
    earlier = blk[None, :] <= blk[:, None]
    kcum = jnp.max(jnp.where(earlier[None, None], kmax[:, :, None, :], 0.0), axis=3)
    fl = ka[:, :, :, HEAD_DIM:HEAD_DIM + F_PARTS].astype(f32).sum(-1)
    f_end = fl[:, :, tq - 1::tq]
    f_start = fl[:, :, ::tq]
    upper = qmax[:, :, :, None] * kcum[:, :, None, :] - f_end[:, :, None, :]
    lower = -qmax * kmax - f_start
    skip = upper < lower[:, :, :, None] - UNDERFLOW_LOG2
    lead = jnp.min(jnp.where(skip, nq, blk[None, None, None, :]), axis=3)
    lead = jnp.minimum(lead, blk[None, None, :])
    return lead.reshape(B, H // hg, hg, nq).min(axis=2).astype(jnp.int32)


def _mixout_kernel(oatt_ref, gm_ref, cv_ref, cvp_ref, x_ref, sgug_ref, wsp_ref, bsp_ref,
                   cw_ref, mg_ref, wout_ref, n2g_ref, xo_ref, h2_ref, *, tb, blocks_per_seq):
    i = pl.program_id(0)
    gm = gm_ref[...]
    u = _gelu(gm[:, :GMLP_WIDTH])
    v = _gelu(gm[:, GMLP_WIDTH:])
    gr = lax.broadcasted_iota(jnp.int32, (GMLP_WIDTH, GMLP_WIDTH), 0) // GROUP_DIM
    gc = lax.broadcasted_iota(jnp.int32, (GMLP_WIDTH, GMLP_WIDTH), 1) // GROUP_DIM
    grp = jnp.where(gr == gc, 1.0 / GROUP_DIM, 0.0).astype(bf16)
    vh, vm, vl = _split3(v * v)
    ms = (jnp.dot(vh, grp, preferred_element_type=f32)
          + jnp.dot(vm, grp, preferred_element_type=f32)
          + jnp.dot(vl, grp, preferred_element_type=f32))
    vn = (v * lax.rsqrt(ms + EPS) * sgug_ref[...]).astype(bf16)
    tr = lax.broadcasted_iota(jnp.int32, (CHUNK, CHUNK), 0)
    tc = lax.broadcasted_iota(jnp.int32, (CHUNK, CHUNK), 1)
    lane_g = lax.broadcasted_iota(jnp.int32, (CHUNK, GMLP_WIDTH), 1) // GROUP_DIM
    mixed_chunks = []
    for c in range(tb // CHUNK):
        vc = vn[c * CHUNK:(c + 1) * CHUNK, :]
        mixed = jnp.zeros((CHUNK, GMLP_WIDTH), f32)
        for g in range(GMLP_GROUPS):
            wg = jnp.where(tc <= tr, wsp_ref[g], 0.0).astype(bf16)
            mg = jnp.dot(wg, vc, preferred_element_type=f32)
            mixed = jnp.where(lane_g == g, mg, mixed)
        mixed_chunks.append(mixed + bsp_ref[...])
    o_gmlp = u * jnp.concatenate(mixed_chunks, axis=0)
    cv = cv_ref[...]
    c_in = cv[:, :CONV_WIDTH]
    c_b = cv[:, CONV_WIDTH:2 * CONV_WIDTH]
    c_c = cv[:, 2 * CONV_WIDTH:]
    z = c_c * c_in
    cvp = cvp_ref[...]
    zp = cvp[:, 2 * CONV_WIDTH:] * cvp[:, :CONV_WIDTH]
    zp = jnp.where(i % blocks_per_seq == 0, jnp.zeros_like(zp), zp)
    rowi = lax.broadcasted_iota(jnp.int32, (tb, CONV_WIDTH), 0)
    z1 = pltpu.roll(z, 1, axis=0)
    z1 = jnp.where(rowi == 0, zp[7:8, :], z1)
    z2 = pltpu.roll(z, 2, axis=0)
    z2 = jnp.where(rowi == 0, zp[6:7, :], z2)
    z2 = jnp.where(rowi == 1, zp[7:8, :], z2)
    cw = cw_ref[...]
    o_conv = c_b * (cw[0:1, :] * z2 + cw[1:2, :] * z1 + cw[2:3, :] * z)
    mg_ = mg_ref[...]
    o_att = oatt_ref[0].T
    mixed_all = jnp.concatenate([
        _rms(o_att, mg_[:, :ATT_WIDTH]),
        _rms(o_gmlp, mg_[:, ATT_WIDTH:ATT_WIDTH + GMLP_WIDTH]),
        _rms(o_conv, mg_[:, ATT_WIDTH + GMLP_WIDTH:])], axis=-1).astype(bf16)
    xn = x_ref[...] + jnp.dot(mixed_all, wout_ref[...], preferred_element_type=f32)
    xo_ref[...] = xn
    h2_ref[...] = _rms(xn, n2g_ref[...]).T.astype(bf16)


def _mix_out(oatt_t, gm, cv, x2, sgug, wsp, bspf, cw, mg, wout, n2g, S, tb):
    T = x2.shape[0]
    bps = S // tb
    const2 = lambda i: (0, 0)
    row = lambda i: (i, 0)
    return pl.pallas_call(
        functools.partial(_mixout_kernel, tb=tb, blocks_per_seq=bps),
        grid=(T // tb,),
        in_specs=[pl.BlockSpec((1, ATT_WIDTH, tb), lambda i: (i // bps, 0, i % bps)),
                  pl.BlockSpec((tb, 2 * GMLP_WIDTH), row),
                  pl.BlockSpec((tb, 3 * CONV_WIDTH), row),
                  pl.BlockSpec((8, 3 * CONV_WIDTH),
                               lambda i: (jnp.maximum(i * (tb // 8) - 1, 0), 0)),
                  pl.BlockSpec((tb, D_MODEL), row),
                  pl.BlockSpec((1, GMLP_WIDTH), const2),
                  pl.BlockSpec((GMLP_GROUPS, CHUNK, CHUNK), lambda i: (0, 0, 0)),
                  pl.BlockSpec((CHUNK, GMLP_WIDTH), const2),
                  pl.BlockSpec((8, CONV_WIDTH), const2),
                  pl.BlockSpec((1, MIX_WIDTH), const2),
                  pl.BlockSpec((MIX_WIDTH, D_MODEL), const2),
                  pl.BlockSpec((1, D_MODEL), const2)],
        out_specs=[pl.BlockSpec((tb, D_MODEL), row),
                   pl.BlockSpec((D_MODEL, tb), lambda i: (0, i))],
        out_shape=[jax.ShapeDtypeStruct((T, D_MODEL), f32),
                   jax.ShapeDtypeStruct((D_MODEL, T), bf16)],
        compiler_params=_cparams(("parallel",)),
        name="mix_out",
    )(oatt_t, gm, cv, cv, x2, sgug, wsp, bspf, cw, mg, wout, n2g)


def _colmax(x):
    return jnp.max(x, axis=0, keepdims=True)


def _colsum(x):
    return jnp.sum(x, axis=0, keepdims=True)


def _tree(op, xs):
    while len(xs) > 1:
        xs = [op(xs[i], xs[i + 1]) if i + 1 < len(xs) else xs[i] for i in range(0, len(xs), 2)]
    return xs[0]


def _top16(s, vals_ref):
    for a in range(PEER_TOPK):
        m = _colmax(s)
        s = jnp.where(s == m, RANK_MARK * (1.0 + a * RANK_STEP), s)
        vals_ref[a:a + 1, :] = m
    rank = jnp.round((s - RANK_MARK) * (1.0 / (RANK_MARK * RANK_STEP)))
    return jnp.where(s <= RANK_MARK, rank, float(PEER_TOPK))


def _select_tables(s1, s2, v1_ref, v2_ref):
    r1 = _top16(s1, v1_ref)
    r2 = _top16(s2, v2_ref)
    v1 = v1_ref[...]
    v2 = v2_ref[...]
    cands = [v1[0:1] + v2[0:8]]
    cands += [v1[a:a + 1] + v2[0:8] for a in range(1, 8)]
    cands += [v1[0:1] + v2[8:16], v1[8:16] + v2[0:1]]
    cmax = v1[0:1] + v2[0:1]
    work = list(cands)
    tau = cmax
    cnt = jnp.zeros_like(cmax)
    for _ in range(PEER_TOPK):
        m = _colmax(_tree(jnp.maximum, work))
        tau = jnp.where(cnt < float(PEER_TOPK), m, tau)
        hits = [w == m for w in work]
        cnt = cnt + _colsum(_tree(jnp.add, [jnp.where(h, 1.0, 0.0) for h in hits]))
        work = [jnp.where(h, -jnp.inf, w) for h, w in zip(hits, work)]
    sel = [c >= tau for c in cands]
    zsum = _colsum(_tree(jnp.add, [jnp.where(s_, jnp.exp(c - cmax), 0.0)
                                   for s_, c in zip(sel, cands)]))
    ones = [jnp.where(s_, 1.0, 0.0) for s_ in sel]
    n_lo = [_colsum(ones[0] + ones[8])] + [_colsum(o) for o in ones[1:8]]
    n_hi = ones[9]
    n1 = jnp.zeros(r1.shape, f32)
    for a in range(8):
        n1 = jnp.where(r1 == float(a), n_lo[a], n1)
    for a in range(8, PEER_TOPK):
        n1 = jnp.where(r1 == float(a), n_hi[a - 8:a - 7], n1)
    e2 = jnp.exp(s2 - v2[0:1])
    c1 = jnp.exp(s1 - v1[0:1]) / zsum
    return r2.astype(bf16), e2.astype(bf16), n1, c1


def _peerprep_kernel(h2_ref, wq_ref, keys_ref, r2_ref, e2_ref, n1_ref, c1_ref,
                     v1_ref, v2_ref):
    qT = jnp.dot(wq_ref[...], h2_ref[...], preferred_element_type=f32)
    qb = qT.astype(bf16)
    s1 = jnp.dot(keys_ref[0], qb[:D_HALF], preferred_element_type=f32)
    s2 = jnp.dot(keys_ref[1], qb[D_HALF:], preferred_element_type=f32)
    lpc = SLAB // LANES
    for c in range(r2_ref.shape[1]):
        r2s, e2s = [], []
        for k in range(lpc):
            t = c * lpc + k
            ls = slice(t * LANES, (t + 1) * LANES)
            r2, e2, n1, c1 = _select_tables(s1[:, ls], s2[:, ls], v1_ref, v2_ref)
            r2s.append(r2)
            e2s.append(e2)
            n1_ref[0, t] = n1
            c1_ref[0, t] = c1
        r2_ref[0, c] = jnp.concatenate(r2s, axis=1)
        e2_ref[0, c] = jnp.concatenate(e2s, axis=1)


def _peer_prep(h2t, wqT, keys, tb):
    T = h2t.shape[1]
    spec = pl.BlockSpec((1, tb // SLAB, N_KEYS, SLAB), lambda t, h: (h, t, 0, 0))
    shp = (PEER_HEADS, T // SLAB, N_KEYS, SLAB)
    rspec = pl.BlockSpec((1, tb // LANES, N_KEYS, LANES), lambda t, h: (h, t, 0, 0))
    rshp = (PEER_HEADS, T // LANES, N_KEYS, LANES)
    return pl.pallas_call(
        _peerprep_kernel,
        grid=(T // tb, PEER_HEADS),
        in_specs=[pl.BlockSpec((D_MODEL, tb), lambda t, h: (0, t)),
                  pl.BlockSpec((D_KEY, D_MODEL), lambda t, h: (h, 0)),
                  pl.BlockSpec((2, N_KEYS, D_HALF), lambda t, h: (0, 0, 0))],
        out_specs=[spec, spec, rspec, rspec],
        out_shape=[jax.ShapeDtypeStruct(shp, bf16), jax.ShapeDtypeStruct(shp, bf16),
                   jax.ShapeDtypeStruct(rshp, f32), jax.ShapeDtypeStruct(rshp, f32)],
        scratch_shapes=[pltpu.VMEM((PEER_TOPK, LANES), f32), pltpu.VMEM((PEER_TOPK, LANES), f32)],
        compiler_params=_cparams(("parallel", "arbitrary")),
        name="peer_prep",
    )(h2t, wqT, keys)


def _peermain_kernel(h2_ref, u_ref, vt_ref, r2_ref, e2_ref, n1_ref, c1_ref, x_ref,
                     o_ref, acc_ref, w_ref, *, rows):
    e = pl.program_id(1)
    tb = h2_ref.shape[1]
    nc = tb // SLAB
    lpc = SLAB // LANES

    @pl.when(e == 0)
    def _():
        acc_ref[...] = jnp.zeros_like(acc_ref)

    base = e * rows

    def row_tile(ref, h, r, c):
        row = jnp.concatenate(
            [jnp.broadcast_to(ref[h, c * lpc + k, pl.ds(base + r, 1), :], (BF16_ROWS, LANES))
             for k in range(lpc)], axis=1).astype(bf16)
        return jnp.tile(row, (N_KEYS // BF16_ROWS, 1))

    aT = jnp.dot(u_ref[...], h2_ref[...], preferred_element_type=f32)
    zero = jnp.zeros((N_KEYS, SLAB), bf16)
    for c in range(nc):
        cs = slice(c * SLAB, (c + 1) * SLAB)
        for r in range(rows):
            rs = slice(r * N_KEYS, (r + 1) * N_KEYS)
            g = None
            for h in range(PEER_HEADS):
                term = jnp.where(r2_ref[h, c] < row_tile(n1_ref, h, r, c), e2_ref[h, c], zero)
                term = term * row_tile(c1_ref, h, r, c)
                g = term if g is None else g + term
            w_ref[c, rs, :] = _gelu(aT[rs, cs]).astype(bf16) * g
    w = jnp.concatenate([w_ref[c] for c in range(nc)], axis=1)
    acc_ref[...] += jnp.dot(vt_ref[...], w, preferred_element_type=f32)

    @pl.when(e == pl.num_programs(1) - 1)
    def _():
        o_ref[...] = x_ref[...] + acc_ref[...].T


def _peer_main(h2t, u, vt, r2, e2, n1, c1, x2, tb, rows):
    T = h2t.shape[1]
    eb = rows * N_KEYS
    sel = pl.BlockSpec((PEER_HEADS, tb // SLAB, N_KEYS, SLAB), lambda t, e: (0, t, 0, 0))
    rsel = pl.BlockSpec((PEER_HEADS, tb // LANES, N_KEYS, LANES), lambda t, e: (0, t, 0, 0))
    return pl.pallas_call(
        functools.partial(_peermain_kernel, rows=rows),
        grid=(T // tb, N_KEYS // rows),
        in_specs=[pl.BlockSpec((D_MODEL, tb), lambda t, e: (0, t)),
                  pl.BlockSpec((eb, D_MODEL), lambda t, e: (e, 0)),
                  pl.BlockSpec((D_MODEL, eb), lambda t, e: (0, e)),
                  sel, sel, rsel, rsel,
                  pl.BlockSpec((tb, D_MODEL), lambda t, e: (t, 0))],
        out_specs=pl.BlockSpec((tb, D_MODEL), lambda t, e: (t, 0)),
        out_shape=jax.ShapeDtypeStruct((T, D_MODEL), f32),
        scratch_shapes=[pltpu.VMEM((D_MODEL, tb), f32),
                        pltpu.VMEM((tb // SLAB, eb, SLAB), bf16)],
        compiler_params=_cparams(("parallel", "arbitrary")),
        name="peer_main",
    )(h2t, u, vt, r2, e2, n1, c1, x2)


def _finalnorm_kernel(x_ref, g_ref, o_ref):
    o_ref[...] = _rms(x_ref[...], g_ref[...])


def _final_norm(x2, g, tb):
    T = x2.shape[0]
    return pl.pallas_call(
        _finalnorm_kernel,
        grid=(T // tb,),
        in_specs=[pl.BlockSpec((tb, D_MODEL), lambda i: (i, 0)),
                  pl.BlockSpec((1, D_MODEL), lambda i: (0, 0))],
        out_specs=pl.BlockSpec((tb, D_MODEL), lambda i: (i, 0)),
        out_shape=jax.ShapeDtypeStruct(x2.shape, f32),
        compiler_params=_cparams(("parallel",)),
        name="final_norm",
    )(x2, g)


def _block_sizes(S):
    return dict(tb_in=min(512, S), tc=min(512, S), tq=min(512, S), heads_per_step=4,
                tb_mix=min(256, S), tb_prep=min(256, S), tb_peer=min(512, S), rows=8)


def kernel(x, norm1_g, w_in, b_forget, sgu_norm_g, w_spatial, b_spatial, conv_w,
           mix_norm_g, w_out, norm2_g, w_query, sub_keys, expert_u, expert_v, final_g):
    B, S, D = x.shape
    T = B * S
    depth = w_in.shape[0]
    bs = _block_sizes(S)
    x2 = x.reshape(T, D)
    for l in range(depth):
        w_main, wf = _in_proj_weights(w_in[l])
        bfp = jnp.pad(b_forget[l], (0, LANES - ATT_HEADS)).reshape(1, LANES)
        qa, kp, vt, gm, cv, logf, nrm = _in_proj(x2, norm1_g[l].reshape(1, D), w_main, wf, bfp,
                                                 B, S, bs["tb_in"])
        ka = _cumsum(logf, kp, bs["tc"])
        jstart = _first_key_block(nrm, ka, bs["tq"], bs["heads_per_step"])
        oatt_t = _attention(jstart, qa, ka, vt, bs["tq"], bs["heads_per_step"])
        bspf = jnp.repeat(b_spatial[l].T, GROUP_DIM, axis=1)
        cw = jnp.pad(conv_w[l], ((0, 8 - CONV_K), (0, 0)))
        x2, h2t = _mix_out(oatt_t, gm, cv, x2, sgu_norm_g[l].reshape(1, GMLP_WIDTH),
                          w_spatial[l], bspf, cw, mix_norm_g[l].reshape(1, MIX_WIDTH),
                          w_out[l].astype(bf16), norm2_g[l].reshape(1, D), S, bs["tb_mix"])
        r2, e2, n1, c1 = _peer_prep(h2t, w_query[l].T.astype(bf16), sub_keys[l].astype(bf16),
                                    bs["tb_prep"])
        x2 = _peer_main(h2t, expert_u[l].astype(bf16), expert_v[l].T.astype(bf16),
                        r2, e2, n1, c1, x2, bs["tb_peer"], bs["rows"])
    return _final_norm(x2, final_g.reshape(1, D), bs["tb_in"]).reshape(B, S, D)
```

```python
import functools
import math

import jax
import jax.numpy as jnp
from jax import lax
from jax.experimental import pallas as pl
from jax.experimental.pallas import tpu as pltpu

f32 = jnp.float32
bf16 = jnp.bfloat16

D_MODEL = 1024
ATT_WIDTH = 512
ATT_HEADS = 8
HEAD_DIM = 64
GMLP_WIDTH = 256
GMLP_GROUPS = 4
GROUP_DIM = 64
CHUNK = 128
CONV_WIDTH = 256
CONV_K = 3
MIX_WIDTH = 1024
PEER_HEADS = 8
N_KEYS = 128
PEER_TOPK = 16
D_KEY = 256
D_HALF = 128
EPS = 1e-6
LANES = 128
BF16_ROWS = 16
SLAB = 256
NEG_BIG = -1e30
LOG2E = math.log2(math.e)
F_PARTS = 3
V_ROWS = 80
UNDERFLOW_LOG2 = 160.0
NORM_SLACK = 1.02
RANK_MARK = -3.0e38
RANK_STEP = 2.0 ** -10

VMEM_LIMIT = 56 * 1024 * 1024

_NT = (((1,), (1,)), ((), ()))


def _cparams(sem, flags=None):
    return pltpu.CompilerParams(dimension_semantics=sem, vmem_limit_bytes=VMEM_LIMIT, flags=flags)


def _gelu(x):
    z2 = (2.0 * 0.7978845608028654) * (x + 0.044715 * (x * x * x))
    return x / (1.0 + jnp.exp(-z2))


def _rms(x, g):
    return x * lax.rsqrt(jnp.mean(x * x, axis=-1, keepdims=True) + EPS) * g


def _split3(x):
    hi = x.astype(bf16)
    r = x - hi.astype(f32)
    mid = r.astype(bf16)
    lo = (r - mid.astype(f32)).astype(bf16)
    return hi, mid, lo


def _inproj_kernel(x_ref, g_ref, w_ref, wf_ref, bf_ref,
                   qa_ref, kp_ref, vt_ref, gm_ref, cv_ref, logf_ref, nrm_ref):
    tb = x_ref.shape[0]
    h = _rms(x_ref[...], g_ref[...])
    hb = h.astype(bf16)
    z = jnp.dot(hb, w_ref[...], preferred_element_type=f32)
    hw = ATT_HEADS * LANES
    lane = lax.broadcasted_iota(jnp.int32, (1, hw), 1) % LANES
    fcol = (lane >= HEAD_DIM) & (lane < HEAD_DIM + F_PARTS)
    zq = z[:, :hw] * (HEAD_DIM ** -0.5 * LOG2E)
    zk = z[:, hw:2 * hw]
    qa = jnp.where(fcol, -1.0, zq).astype(bf16)
    kp = zk.astype(bf16)
    vT = z[:, 2 * hw:2 * hw + ATT_WIDTH].T
    tail = jnp.where(lax.broadcasted_iota(jnp.int32, (V_ROWS - HEAD_DIM, tb), 0) == 0,
                     1.0, 0.0).astype(bf16)
    for hd in range(ATT_HEADS):
        qa_ref[0, hd] = qa[:, hd * LANES:(hd + 1) * LANES]
        kp_ref[0, hd] = kp[:, hd * LANES:(hd + 1) * LANES]
        vt_ref[0, hd, :HEAD_DIM, :] = vT[hd * HEAD_DIM:(hd + 1) * HEAD_DIM, :].astype(bf16)
        vt_ref[0, hd, HEAD_DIM:, :] = tail
    c0 = 2 * hw + ATT_WIDTH
    gm_ref[...] = z[:, c0:c0 + 2 * GMLP_WIDTH]
    cv_ref[...] = z[:, c0 + 2 * GMLP_WIDTH:]
    r = lax.broadcasted_iota(jnp.int32, (hw, LANES), 0) // LANES
    c = lax.broadcasted_iota(jnp.int32, (hw, LANES), 1)
    ind_q = jnp.where(r == c, 1.0, 0.0).astype(bf16)
    ind_k = jnp.where(r + ATT_HEADS == c, 1.0, 0.0).astype(bf16)
    nrm_ref[...] = (jnp.dot((zq * zq).astype(bf16), ind_q, preferred_element_type=f32)
                    + jnp.dot((zk * zk).astype(bf16), ind_k, preferred_element_type=f32))
    hl = (h - hb.astype(f32)).astype(bf16)
    wf = wf_ref[...]
    wfh = wf.astype(bf16)
    wfl = (wf - wfh.astype(f32)).astype(bf16)
    zf = (jnp.dot(hb, wfh, preferred_element_type=f32)
          + jnp.dot(hb, wfl, preferred_element_type=f32)
          + jnp.dot(hl, wfh, preferred_element_type=f32)) + bf_ref[...]
    logf_ref[...] = jnp.minimum(zf, 0.0) - jnp.log1p(jnp.exp(-jnp.abs(zf)))


def _in_proj(x2, g, w, wf, bfp, B, S, tb):
    T = x2.shape[0]
    bps = S // tb
    ncol = w.shape[1]
    const = lambda i: (0, 0)
    row = lambda i: (i, 0)
    head_spec = pl.BlockSpec((1, ATT_HEADS, tb, LANES), lambda i: (i // bps, 0, i % bps, 0))
    head_shape = jax.ShapeDtypeStruct((B, ATT_HEADS, S, LANES), bf16)
    return pl.pallas_call(
        _inproj_kernel,
        grid=(T // tb,),
        in_specs=[pl.BlockSpec((tb, D_MODEL), row),
                  pl.BlockSpec((1, D_MODEL), const),
                  pl.BlockSpec((D_MODEL, ncol), const),
                  pl.BlockSpec((D_MODEL, LANES), const),
                  pl.BlockSpec((1, LANES), const)],
        out_specs=[head_spec, head_spec,
                   pl.BlockSpec((1, ATT_HEADS, V_ROWS, tb), lambda i: (i // bps, 0, 0, i % bps)),
                   pl.BlockSpec((tb, 2 * GMLP_WIDTH), row),
                   pl.BlockSpec((tb, 3 * CONV_WIDTH), row),
                   pl.BlockSpec((tb, LANES), row),
                   pl.BlockSpec((tb, LANES), row)],
        out_shape=[head_shape, head_shape,
                   jax.ShapeDtypeStruct((B, ATT_HEADS, V_ROWS, S), bf16),
                   jax.ShapeDtypeStruct((T, 2 * GMLP_WIDTH), f32),
                   jax.ShapeDtypeStruct((T, 3 * CONV_WIDTH), f32),
                   jax.ShapeDtypeStruct((T, LANES), f32),
                   jax.ShapeDtypeStruct((T, LANES), f32)],
        compiler_params=_cparams(("parallel",)),
        name="in_proj",
    )(x2, g, w, wf, bfp)


def _in_proj_weights(wl):
    f0 = 3 * ATT_WIDTH

    def per_head(cols):
        w3 = cols.reshape(D_MODEL, ATT_HEADS, HEAD_DIM)
        return jnp.pad(w3, ((0, 0), (0, 0), (0, LANES - HEAD_DIM))).reshape(D_MODEL, -1)

    w_main = jnp.concatenate([per_head(wl[:, :ATT_WIDTH]), per_head(wl[:, ATT_WIDTH:2 * ATT_WIDTH]),
                              wl[:, 2 * ATT_WIDTH:f0], wl[:, f0 + ATT_HEADS:]], axis=1)
    wf = jnp.pad(wl[:, f0:f0 + ATT_HEADS], ((0, 0), (0, LANES - ATT_HEADS)))
    return w_main.astype(bf16), wf


def _cumsum_kernel(x_ref, kp_ref, ka_ref, carry_ref, *, tc):
    @pl.when(pl.program_id(1) == 0)
    def _():
        carry_ref[...] = jnp.zeros_like(carry_ref)

    row = lax.broadcasted_iota(jnp.int32, (tc, tc), 0)
    col = lax.broadcasted_iota(jnp.int32, (tc, tc), 1)
    tri = jnp.where(col <= row, 1.0, 0.0).astype(bf16)
    hi, mid, lo = _split3(x_ref[...])
    c = (jnp.dot(tri, hi, preferred_element_type=f32)
         + jnp.dot(tri, mid, preferred_element_type=f32)
         + jnp.dot(tri, lo, preferred_element_type=f32)) + carry_ref[...]
    carry_ref[...] = c[tc - 1:tc, :]
    hw = ATT_HEADS * LANES
    src = lax.broadcasted_iota(jnp.int32, (LANES, hw), 0)
    dst = lax.broadcasted_iota(jnp.int32, (LANES, hw), 1)
    fcols = jnp.zeros((tc, hw), f32)
    for p, part in enumerate(_split3(c * LOG2E)):
        place = jnp.where(dst == src * LANES + HEAD_DIM + p, 1.0, 0.0).astype(bf16)
        fcols = fcols + jnp.dot(part, place, preferred_element_type=f32)
    for hd in range(ATT_HEADS):
        ka_ref[0, hd] = (kp_ref[0, hd].astype(f32)
                         + fcols[:, hd * LANES:(hd + 1) * LANES]).astype(bf16)


def _cumsum(logf, kp, tc):
    B, H, S, _ = kp.shape
    nb = S // tc
    head_spec = pl.BlockSpec((1, H, tc, LANES), lambda b, j: (b, 0, j, 0))
    return pl.pallas_call(
        functools.partial(_cumsum_kernel, tc=tc),
        grid=(B, nb),
        in_specs=[pl.BlockSpec((tc, LANES), lambda b, j: (b * nb + j, 0)), head_spec],
        out_specs=head_spec,
        out_shape=jax.ShapeDtypeStruct(kp.shape, bf16),
        scratch_shapes=[pltpu.VMEM((1, LANES), f32)],
        compiler_params=_cparams(("arbitrary", "arbitrary")),
        name="cumsum_logf",
    )(logf, kp)


def _attn_kernel(jstart_ref, qa_ref, ka_ref, vt_ref, o_ref, acc_ref, m_ref, *, tq, hg):
    i = pl.program_id(2)
    m_ref[...] = jnp.full(m_ref.shape, NEG_BIG, f32)
    acc_ref[...] = jnp.zeros(acc_ref.shape, f32)

    def step(j, masked):
        start = pl.multiple_of(j * tq, tq)
        for h in range(hg):
            k = ka_ref[0, h, pl.ds(start, tq), :]
            s = lax.dot_general(k, qa_ref[0, h], _NT, preferred_element_type=f32)
            if masked:
                r = lax.broadcasted_iota(jnp.int32, (tq, tq), 0)
                c = lax.broadcasted_iota(jnp.int32, (tq, tq), 1)
                s = jnp.where(r <= c, s, NEG_BIG)
            m = m_ref[h]
            m_new = jnp.maximum(m, jnp.max(s, axis=0, keepdims=True))
            p = jnp.exp2(s - m_new).astype(bf16)
            pv = jnp.dot(vt_ref[0, h, :, pl.ds(start, tq)], p, preferred_element_type=f32)
            acc_ref[h] = jnp.exp2(m - m_new) * acc_ref[h] + pv
            m_ref[h] = m_new

    def body(j, carry):
        step(j, False)
        return carry

    lax.fori_loop(jstart_ref[pl.program_id(0), pl.program_id(1), i], i, body, 0)
    step(i, True)
    for h in range(hg):
        o_ref[0, h * HEAD_DIM:(h + 1) * HEAD_DIM, :] = (
            acc_ref[h, :HEAD_DIM, :] / acc_ref[h, HEAD_DIM:HEAD_DIM + 1, :])


def _attention(jstart, qa, ka, vt, tq, hg):
    B, H, S, _ = qa.shape
    return pl.pallas_call(
        functools.partial(_attn_kernel, tq=tq, hg=hg),
        grid_spec=pltpu.PrefetchScalarGridSpec(
            num_scalar_prefetch=1,
            grid=(B, H // hg, S // tq),
            in_specs=[pl.BlockSpec((1, hg, tq, LANES), lambda b, g, i, js: (b, g, i, 0)),
                      pl.BlockSpec((1, hg, S, LANES), lambda b, g, i, js: (b, g, 0, 0)),
                      pl.BlockSpec((1, hg, V_ROWS, S), lambda b, g, i, js: (b, g, 0, 0))],
            out_specs=pl.BlockSpec((1, hg * HEAD_DIM, tq), lambda b, g, i, js: (b, g, i)),
            scratch_shapes=[pltpu.VMEM((hg, V_ROWS, tq), f32), pltpu.VMEM((hg, 1, tq), f32)]),
        out_shape=jax.ShapeDtypeStruct((B, H * HEAD_DIM, S), f32),
        compiler_params=_cparams(("parallel", "parallel", "arbitrary")),
        name="fox_attention",
    )(jstart, qa, ka, vt)


def _first_key_block(nrm, ka, tq, hg):
    B, H, S, _ = ka.shape
    nq = S // tq
    n2 = jnp.max(nrm.reshape(B, nq, tq, LANES), axis=2)
    qmax = jnp.sqrt(n2[:, :, :H]).transpose(0, 2, 1) * NORM_SLACK
    kmax = jnp.sqrt(n2[:, :, H:2 * H]).transpose(0, 2, 1) * NORM_SLACK
    blk = jnp.arange(nq, dtype=jnp.int32)
    earlier = blk[None, :] <= blk[:, None]
    kcum = jnp.max(jnp.where(earlier[None, None], kmax[:, :, None, :], 0.0), axis=3)
    fl = ka[:, :, :, HEAD_DIM:HEAD_DIM + F_PARTS].astype(f32).sum(-1)
    f_end = fl[:, :, tq - 1::tq]
    f_start = fl[:, :, ::tq]
    upper = qmax[:, :, :, None] * kcum[:, :, None, :] - f_end[:, :, None, :]
    lower = -qmax * kmax - f_start
    skip = upper < lower[:, :, :, None] - UNDERFLOW_LOG2
    lead = jnp.min(jnp.where(skip, nq, blk[None, None, None, :]), axis=3)
    lead = jnp.minimum(lead, blk[None, None, :])
    return lead.reshape(B, H // hg, hg, nq).min(axis=2).astype(jnp.int32)


def _mixout_kernel(oatt_ref, gm_ref, cv_ref, cvp_ref, x_ref, sgug_ref, wsp_ref, bsp_ref,
                   cw_ref, mg_ref, wout_ref, n2g_ref, xo_ref, h2_ref, *, tb, blocks_per_seq):
    i = pl.program_id(0)
    gm = gm_ref[...]
    u = _gelu(gm[:, :GMLP_WIDTH])
    v = _gelu(gm[:, GMLP_WIDTH:])
    gr = lax.broadcasted_iota(jnp.int32, (GMLP_WIDTH, GMLP_WIDTH), 0) // GROUP_DIM
    gc = lax.broadcasted_iota(jnp.int32, (GMLP_WIDTH, GMLP_WIDTH), 1) // GROUP_DIM
    grp = jnp.where(gr == gc, 1.0 / GROUP_DIM, 0.0).astype(bf16)
    vh, vm, vl = _split3(v * v)
    ms = (jnp.dot(vh, grp, preferred_element_type=f32)
          + jnp.dot(vm, grp, preferred_element_type=f32)
          + jnp.dot(vl, grp, preferred_element_type=f32))
    vn = (v * lax.rsqrt(ms + EPS) * sgug_ref[...]).astype(bf16)
    tr = lax.broadcasted_iota(jnp.int32, (CHUNK, CHUNK), 0)
    tc = lax.broadcasted_iota(jnp.int32, (CHUNK, CHUNK), 1)
    lane_g = lax.broadcasted_iota(jnp.int32, (CHUNK, GMLP_WIDTH), 1) // GROUP_DIM
    mixed_chunks = []
    for c in range(tb // CHUNK):
        vc = vn[c * CHUNK:(c + 1) * CHUNK, :]
        mixed = jnp.zeros((CHUNK, GMLP_WIDTH), f32)
        for g in range(GMLP_GROUPS):
            wg = jnp.where(tc <= tr, wsp_ref[g], 0.0).astype(bf16)
            mg = jnp.dot(wg, vc, preferred_element_type=f32)
            mixed = jnp.where(lane_g == g, mg, mixed)
        mixed_chunks.append(mixed + bsp_ref[...])
    o_gmlp = u * jnp.concatenate(mixed_chunks, axis=0)
    cv = cv_ref[...]
    c_in = cv[:, :CONV_WIDTH]
    c_b = cv[:, CONV_WIDTH:2 * CONV_WIDTH]
    c_c = cv[:, 2 * CONV_WIDTH:]
    z = c_c * c_in
    cvp = cvp_ref[...]
    zp = cvp[:, 2 * CONV_WIDTH:] * cvp[:, :CONV_WIDTH]
    zp = jnp.where(i % blocks_per_seq == 0, jnp.zeros_like(zp), zp)
    rowi = lax.broadcasted_iota(jnp.int32, (tb, CONV_WIDTH), 0)
    z1 = pltpu.roll(z, 1, axis=0)
    z1 = jnp.where(rowi == 0, zp[7:8, :], z1)
    z2 = pltpu.roll(z, 2, axis=0)
    z2 = jnp.where(rowi == 0, zp[6:7, :], z2)
    z2 = jnp.where(rowi == 1, zp[7:8, :], z2)
    cw = cw_ref[...]
    o_conv = c_b * (cw[0:1, :] * z2 + cw[1:2, :] * z1 + cw[2:3, :] * z)
    mg_ = mg_ref[...]
    o_att = oatt_ref[0].T
    mixed_all = jnp.concatenate([
        _rms(o_att, mg_[:, :ATT_WIDTH]),
        _rms(o_gmlp, mg_[:, ATT_WIDTH:ATT_WIDTH + GMLP_WIDTH]),
        _rms(o_conv, mg_[:, ATT_WIDTH + GMLP_WIDTH:])], axis=-1).astype(bf16)
    xn = x_ref[...] + jnp.dot(mixed_all, wout_ref[...], preferred_element_type=f32)
    xo_ref[...] = xn
    h2_ref[...] = _rms(xn, n2g_ref[...]).T.astype(bf16)


def _mix_out(oatt_t, gm, cv, x2, sgug, wsp, bspf, cw, mg, wout, n2g, S, tb):
    T = x2.shape[0]
    bps = S // tb
    const2 = lambda i: (0, 0)
    row = lambda i: (i, 0)
    return pl.pallas_call(
        functools.partial(_mixout_kernel, tb=tb, blocks_per_seq=bps),
        grid=(T // tb,),
        in_specs=[pl.BlockSpec((1, ATT_WIDTH, tb), lambda i: (i // bps, 0, i % bps)),
                  pl.BlockSpec((tb, 2 * GMLP_WIDTH), row),
                  pl.BlockSpec((tb, 3 * CONV_WIDTH), row),
                  pl.BlockSpec((8, 3 * CONV_WIDTH),
                               lambda i: (jnp.maximum(i * (tb // 8) - 1, 0), 0)),
                  pl.BlockSpec((tb, D_MODEL), row),
                  pl.BlockSpec((1, GMLP_WIDTH), const2),
                  pl.BlockSpec((GMLP_GROUPS, CHUNK, CHUNK), lambda i: (0, 0, 0)),
                  pl.BlockSpec((CHUNK, GMLP_WIDTH), const2),
                  pl.BlockSpec((8, CONV_WIDTH), const2),
                  pl.BlockSpec((1, MIX_WIDTH), const2),
                  pl.BlockSpec((MIX_WIDTH, D_MODEL), const2),
                  pl.BlockSpec((1, D_MODEL), const2)],
        out_specs=[pl.BlockSpec((tb, D_MODEL), row),
                   pl.BlockSpec((D_MODEL, tb), lambda i: (0, i))],
        out_shape=[jax.ShapeDtypeStruct((T, D_MODEL), f32),
                   jax.ShapeDtypeStruct((D_MODEL, T), bf16)],
        compiler_params=_cparams(("parallel",)),
        name="mix_out",
    )(oatt_t, gm, cv, cv, x2, sgug, wsp, bspf, cw, mg, wout, n2g)


def _colmax(x):
    return jnp.max(x, axis=0, keepdims=True)


def _colsum(x):
    return jnp.sum(x, axis=0, keepdims=True)


def _tree(op, xs):
    while len(xs) > 1:
        xs = [op(xs[i], xs[i + 1]) if i + 1 < len(xs) else xs[i] for i in range(0, len(xs), 2)]
    return xs[0]


def _top16(s, vals_ref):
    for a in range(PEER_TOPK):
        m = _colmax(s)
        s = jnp.where(s == m, RANK_MARK * (1.0 + a * RANK_STEP), s)
        vals_ref[a:a + 1, :] = m
    rank = jnp.round((s - RANK_MARK) * (1.0 / (RANK_MARK * RANK_STEP)))
    return jnp.where(s <= RANK_MARK, rank, float(PEER_TOPK))


def _select_tables(s1, s2, v1_ref, v2_ref):
    r1 = _top16(s1, v1_ref)
    r2 = _top16(s2, v2_ref)
    v1 = v1_ref[...]
    v2 = v2_ref[...]
    cands = [v1[0:1] + v2[0:8]]
    cands += [v1[a:a + 1] + v2[0:8] for a in range(1, 8)]
    cands += [v1[0:1] + v2[8:16], v1[8:16] + v2[0:1]]
    cmax = v1[0:1] + v2[0:1]
    work = list(cands)
    tau = cmax
    cnt = jnp.zeros_like(cmax)
    for _ in range(PEER_TOPK):
        m = _colmax(_tree(jnp.maximum, work))
        tau = jnp.where(cnt < float(PEER_TOPK), m, tau)
        hits = [w == m for w in work]
        cnt = cnt + _colsum(_tree(jnp.add, [jnp.where(h, 1.0, 0.0) for h in hits]))
        work = [jnp.where(h, -jnp.inf, w) for h, w in zip(hits, work)]
    sel = [c >= tau for c in cands]
    zsum = _colsum(_tree(jnp.add, [jnp.where(s_, jnp.exp(c - cmax), 0.0)
                                   for s_, c in zip(sel, cands)]))
    ones = [jnp.where(s_, 1.0, 0.0) for s_ in sel]
    n_lo = [_colsum(ones[0] + ones[8])] + [_colsum(o) for o in ones[1:8]]
    n_hi = ones[9]
    n1 = jnp.zeros(r1.shape, f32)
    for a in range(8):
        n1 = jnp.where(r1 == float(a), n_lo[a], n1)
    for a in range(8, PEER_TOPK):
        n1 = jnp.where(r1 == float(a), n_hi[a - 8:a - 7], n1)
    e2 = jnp.exp(s2 - v2[0:1])
    c1 = jnp.exp(s1 - v1[0:1]) / zsum
    return r2.astype(bf16), e2.astype(bf16), n1, c1


def _peerprep_kernel(h2_ref, wq_ref, keys_ref, r2_ref, e2_ref, n1_ref, c1_ref,
                     v1_ref, v2_ref):
    qT = jnp.dot(wq_ref[...], h2_ref[...], preferred_element_type=f32)
    qb = qT.astype(bf16)
    s1 = jnp.dot(keys_ref[0], qb[:D_HALF], preferred_element_type=f32)
    s2 = jnp.dot(keys_ref[1], qb[D_HALF:], preferred_element_type=f32)
    lpc = SLAB // LANES
    for c in range(r2_ref.shape[1]):
        r2s, e2s = [], []
        for k in range(lpc):
            t = c * lpc + k
            ls = slice(t * LANES, (t + 1) * LANES)
            r2, e2, n1, c1 = _select_tables(s1[:, ls], s2[:, ls], v1_ref, v2_ref)
            r2s.append(r2)
            e2s.append(e2)
            n1_ref[0, t] = n1
            c1_ref[0, t] = c1
        r2_ref[0, c] = jnp.concatenate(r2s, axis=1)
        e2_ref[0, c] = jnp.concatenate(e2s, axis=1)


def _peer_prep(h2t, wqT, keys, tb):
    T = h2t.shape[1]
    spec = pl.BlockSpec((1, tb // SLAB, N_KEYS, SLAB), lambda t, h: (h, t, 0, 0))
    shp = (PEER_HEADS, T // SLAB, N_KEYS, SLAB)
    rspec = pl.BlockSpec((1, tb // LANES, N_KEYS, LANES), lambda t, h: (h, t, 0, 0))
    rshp = (PEER_HEADS, T // LANES, N_KEYS, LANES)
    return pl.pallas_call(
        _peerprep_kernel,
        grid=(T // tb, PEER_HEADS),
        in_specs=[pl.BlockSpec((D_MODEL, tb), lambda t, h: (0, t)),
                  pl.BlockSpec((D_KEY, D_MODEL), lambda t, h: (h, 0)),
                  pl.BlockSpec((2, N_KEYS, D_HALF), lambda t, h: (0, 0, 0))],
        out_specs=[spec, spec, rspec, rspec],
        out_shape=[jax.ShapeDtypeStruct(shp, bf16), jax.ShapeDtypeStruct(shp, bf16),
                   jax.ShapeDtypeStruct(rshp, f32), jax.ShapeDtypeStruct(rshp, f32)],
        scratch_shapes=[pltpu.VMEM((PEER_TOPK, LANES), f32), pltpu.VMEM((PEER_TOPK, LANES), f32)],
        compiler_params=_cparams(("parallel", "arbitrary")),
        name="peer_prep",
    )(h2t, wqT, keys)


def _peermain_kernel(h2_ref, u_ref, vt_ref, r2_ref, e2_ref, n1_ref, c1_ref, x_ref,
                     o_ref, acc_ref, w_ref, *, rows):
    e = pl.program_id(1)
    tb = h2_ref.shape[1]
    nc = tb // SLAB
    lpc = SLAB // LANES

    @pl.when(e == 0)
    def _():
        acc_ref[...] = jnp.zeros_like(acc_ref)

    base = e * rows

    def row_tile(ref, h, r, c):
        row = jnp.concatenate(
            [jnp.broadcast_to(ref[h, c * lpc + k, pl.ds(base + r, 1), :], (BF16_ROWS, LANES))
             for k in range(lpc)], axis=1).astype(bf16)
        return jnp.tile(row, (N_KEYS // BF16_ROWS, 1))

    zero = jnp.zeros((N_KEYS, SLAB), bf16)

    def gate_slab(c, carry):
        for r in range(rows):
            g = None
            for h in range(PEER_HEADS):
                term = jnp.where(r2_ref[h, c] < row_tile(n1_ref, h, r, c), e2_ref[h, c], zero)
                term = term * row_tile(c1_ref, h, r, c)
                g = term if g is None else g + term
            w_ref[c, r * N_KEYS:(r + 1) * N_KEYS, :] = g
        return carry

    lax.fori_loop(0, nc, gate_slab, 0)
    gates = jnp.concatenate([w_ref[c] for c in range(nc)], axis=1)
    aT = jnp.dot(u_ref[...], h2_ref[...], preferred_element_type=f32)
    w = _gelu(aT).astype(bf16) * gates
    acc_ref[...] += jnp.dot(vt_ref[...], w, preferred_element_type=f32)

    @pl.when(e == pl.num_programs(1) - 1)
    def _():
        o_ref[...] = x_ref[...] + acc_ref[...].T


def _peer_main(h2t, u, vt, r2, e2, n1, c1, x2, tb, rows):
    T = h2t.shape[1]
    eb = rows * N_KEYS
    sel = pl.BlockSpec((PEER_HEADS, tb // SLAB, N_KEYS, SLAB), lambda t, e: (0, t, 0, 0))
    rsel = pl.BlockSpec((PEER_HEADS, tb // LANES, N_KEYS, LANES), lambda t, e: (0, t, 0, 0))
    return pl.pallas_call(
        functools.partial(_peermain_kernel, rows=rows),
        grid=(T // tb, N_KEYS // rows),
        in_specs=[pl.BlockSpec((D_MODEL, tb), lambda t, e: (0, t)),
                  pl.BlockSpec((eb, D_MODEL), lambda t, e: (e, 0)),
                  pl.BlockSpec((D_MODEL, eb), lambda t, e: (0, e)),
                  sel, sel, rsel, rsel,
                  pl.BlockSpec((tb, D_MODEL), lambda t, e: (t, 0))],
        out_specs=pl.BlockSpec((tb, D_MODEL), lambda t, e: (t, 0)),
        out_shape=jax.ShapeDtypeStruct((T, D_MODEL), f32),
        scratch_shapes=[pltpu.VMEM((D_MODEL, tb), f32),
                        pltpu.VMEM((tb // SLAB, eb, SLAB), bf16)],
        compiler_params=_cparams(("parallel", "arbitrary")),
        name="peer_main",
    )(h2t, u, vt, r2, e2, n1, c1, x2)


def _finalnorm_kernel(x_ref, g_ref, o_ref):
    o_ref[...] = _rms(x_ref[...], g_ref[...])


def _final_norm(x2, g, tb):
    T = x2.shape[0]
    return pl.pallas_call(
        _finalnorm_kernel,
        grid=(T // tb,),
        in_specs=[pl.BlockSpec((tb, D_MODEL), lambda i: (i, 0)),
                  pl.BlockSpec((1, D_MODEL), lambda i: (0, 0))],
        out_specs=pl.BlockSpec((tb, D_MODEL), lambda i: (i, 0)),
        out_shape=jax.ShapeDtypeStruct(x2.shape, f32),
        compiler_params=_cparams(("parallel",)),
        name="final_norm",
    )(x2, g)


def _block_sizes(S):
    return dict(tb_in=min(512, S), tc=min(512, S), tq=min(512, S), heads_per_step=4,
                tb_mix=min(256, S), tb_prep=min(256, S), tb_peer=min(512, S), rows=8)


def kernel(x, norm1_g, w_in, b_forget, sgu_norm_g, w_spatial, b_spatial, conv_w,
           mix_norm_g, w_out, norm2_g, w_query, sub_keys, expert_u, expert_v, final_g):
    B, S, D = x.shape
    T = B * S
    depth = w_in.shape[0]
    bs = _block_sizes(S)
    x2 = x.reshape(T, D)
    for l in range(depth):
        w_main, wf = _in_proj_weights(w_in[l])
        bfp = jnp.pad(b_forget[l], (0, LANES - ATT_HEADS)).reshape(1, LANES)
        qa, kp, vt, gm, cv, logf, nrm = _in_proj(x2, norm1_g[l].reshape(1, D), w_main, wf, bfp,
                                                 B, S, bs["tb_in"])
        ka = _cumsum(logf, kp, bs["tc"])
        jstart = _first_key_block(nrm, ka, bs["tq"], bs["heads_per_step"])
        oatt_t = _attention(jstart, qa, ka, vt, bs["tq"], bs["heads_per_step"])
        bspf = jnp.repeat(b_spatial[l].T, GROUP_DIM, axis=1)
        cw = jnp.pad(conv_w[l], ((0, 8 - CONV_K), (0, 0)))
        x2, h2t = _mix_out(oatt_t, gm, cv, x2, sgu_norm_g[l].reshape(1, GMLP_WIDTH),
                          w_spatial[l], bspf, cw, mix_norm_g[l].reshape(1, MIX_WIDTH),
                          w_out[l].astype(bf16), norm2_g[l].reshape(1, D), S, bs["tb_mix"])
        r2, e2, n1, c1 = _peer_prep(h2t, w_query[l].T.astype(bf16), sub_keys[l].astype(bf16),
                                    bs["tb_prep"])
        x2 = _peer_main(h2t, expert_u[l].astype(bf16), expert_v[l].T.astype(bf16),
                        r2, e2, n1, c1, x2, bs["tb_peer"], bs["rows"])
    return _final_norm(x2, final_g.reshape(1, D), bs["tb_in"]).reshape(B, S, D)
```

```python
import functools
import math

import jax
import jax.numpy as jnp
from jax import lax
from jax.experimental import pallas as pl
from jax.experimental.pallas import tpu as pltpu

f32 = jnp.float32
bf16 = jnp.bfloat16

D_MODEL = 1024
ATT_WIDTH = 512
ATT_HEADS = 8
HEAD_DIM = 64
GMLP_WIDTH = 256
GMLP_GROUPS = 4
GROUP_DIM = 64
CHUNK = 128
CONV_WIDTH = 256
CONV_K = 3
MIX_WIDTH = 1024
PEER_HEADS = 8
N_KEYS = 128
PEER_TOPK = 16
D_KEY = 256
D_HALF = 128
EPS = 1e-6
LANES = 128
BF16_ROWS = 16
SLAB = 256
NEG_BIG = -1e30
LOG2E = math.log2(math.e)
F_PARTS = 3
V_ROWS = 80
UNDERFLOW_LOG2 = 160.0
NORM_SLACK = 1.02
RANK_MARK = -3.0e38
RANK_STEP = 2.0 ** -10

VMEM_LIMIT = 56 * 1024 * 1024

_NT = (((1,), (1,)), ((), ()))


def _cparams(sem, flags=None):
    return pltpu.CompilerParams(dimension_semantics=sem, vmem_limit_bytes=VMEM_LIMIT, flags=flags)


def _gelu(x):
    z2 = (2.0 * 0.7978845608028654) * (x + 0.044715 * (x * x * x))
    return x / (1.0 + jnp.exp(-z2))


def _rms(x, g):
    return x * lax.rsqrt(jnp.mean(x * x, axis=-1, keepdims=True) + EPS) * g


def _split3(x):
    hi = x.astype(bf16)
    r = x - hi.astype(f32)
    mid = r.astype(bf16)
    lo = (r - mid.astype(f32)).astype(bf16)
    return hi, mid, lo


def _inproj_kernel(x_ref, g_ref, w_ref, wf_ref, bf_ref,
                   qa_ref, kp_ref, vt_ref, gm_ref, cv_ref, logf_ref, nrm_ref):
    tb = x_ref.shape[0]
    h = _rms(x_ref[...], g_ref[...])
    hb = h.astype(bf16)
    z = jnp.dot(hb, w_ref[...], preferred_element_type=f32)
    hw = ATT_HEADS * LANES
    lane = lax.broadcasted_iota(jnp.int32, (1, hw), 1) % LANES
    fcol = (lane >= HEAD_DIM) & (lane < HEAD_DIM + F_PARTS)
    zq = z[:, :hw] * (HEAD_DIM ** -0.5 * LOG2E)
    zk = z[:, hw:2 * hw]
    qa = jnp.where(fcol, -1.0, zq).astype(bf16)
    kp = zk.astype(bf16)
    vT = z[:, 2 * hw:2 * hw + ATT_WIDTH].T
    tail = jnp.where(lax.broadcasted_iota(jnp.int32, (V_ROWS - HEAD_DIM, tb), 0) == 0,
                     1.0, 0.0).astype(bf16)
    for hd in range(ATT_HEADS):
        qa_ref[0, hd] = qa[:, hd * LANES:(hd + 1) * LANES]
        kp_ref[0, hd] = kp[:, hd * LANES:(hd + 1) * LANES]
        vt_ref[0, hd, :HEAD_DIM, :] = vT[hd * HEAD_DIM:(hd + 1) * HEAD_DIM, :].astype(bf16)
        vt_ref[0, hd, HEAD_DIM:, :] = tail
    c0 = 2 * hw + ATT_WIDTH
    gm_ref[...] = z[:, c0:c0 + 2 * GMLP_WIDTH]
    cv_ref[...] = z[:, c0 + 2 * GMLP_WIDTH:]
    r = lax.broadcasted_iota(jnp.int32, (hw, LANES), 0) // LANES
    c = lax.broadcasted_iota(jnp.int32, (hw, LANES), 1)
    ind_q = jnp.where(r == c, 1.0, 0.0).astype(bf16)
    ind_k = jnp.where(r + ATT_HEADS == c, 1.0, 0.0).astype(bf16)
    nrm_ref[...] = (jnp.dot((zq * zq).astype(bf16), ind_q, preferred_element_type=f32)
                    + jnp.dot((zk * zk).astype(bf16), ind_k, preferred_element_type=f32))
    hl = (h - hb.astype(f32)).astype(bf16)
    wf = wf_ref[...]
    wfh = wf.astype(bf16)
    wfl = (wf - wfh.astype(f32)).astype(bf16)
    zf = (jnp.dot(hb, wfh, preferred_element_type=f32)
          + jnp.dot(hb, wfl, preferred_element_type=f32)
          + jnp.dot(hl, wfh, preferred_element_type=f32)) + bf_ref[...]
    logf_ref[...] = jnp.minimum(zf, 0.0) - jnp.log1p(jnp.exp(-jnp.abs(zf)))


def _in_proj(x2, g, w, wf, bfp, B, S, tb):
    T = x2.shape[0]
    bps = S // tb
    ncol = w.shape[1]
    const = lambda i: (0, 0)
    row = lambda i: (i, 0)
    head_spec = pl.BlockSpec((1, ATT_HEADS, tb, LANES), lambda i: (i // bps, 0, i % bps, 0))
    head_shape = jax.ShapeDtypeStruct((B, ATT_HEADS, S, LANES), bf16)
    return pl.pallas_call(
        _inproj_kernel,
        grid=(T // tb,),
        in_specs=[pl.BlockSpec((tb, D_MODEL), row),
                  pl.BlockSpec((1, D_MODEL), const),
                  pl.BlockSpec((D_MODEL, ncol), const),
                  pl.BlockSpec((D_MODEL, LANES), const),
                  pl.BlockSpec((1, LANES), const)],
        out_specs=[head_spec, head_spec,
                   pl.BlockSpec((1, ATT_HEADS, V_ROWS, tb), lambda i: (i // bps, 0, 0, i % bps)),
                   pl.BlockSpec((tb, 2 * GMLP_WIDTH), row),
                   pl.BlockSpec((tb, 3 * CONV_WIDTH), row),
                   pl.BlockSpec((tb, LANES), row),
                   pl.BlockSpec((tb, LANES), row)],
        out_shape=[head_shape, head_shape,
                   jax.ShapeDtypeStruct((B, ATT_HEADS, V_ROWS, S), bf16),
                   jax.ShapeDtypeStruct((T, 2 * GMLP_WIDTH), f32),
                   jax.ShapeDtypeStruct((T, 3 * CONV_WIDTH), f32),
                   jax.ShapeDtypeStruct((T, LANES), f32),
                   jax.ShapeDtypeStruct((T, LANES), f32)],
        compiler_params=_cparams(("parallel",)),
        name="in_proj",
    )(x2, g, w, wf, bfp)


def _in_proj_weights(wl):
    f0 = 3 * ATT_WIDTH

    def per_head(cols):
        w3 = cols.reshape(D_MODEL, ATT_HEADS, HEAD_DIM)
        return jnp.pad(w3, ((0, 0), (0, 0), (0, LANES - HEAD_DIM))).reshape(D_MODEL, -1)

    w_main = jnp.concatenate([per_head(wl[:, :ATT_WIDTH]), per_head(wl[:, ATT_WIDTH:2 * ATT_WIDTH]),
                              wl[:, 2 * ATT_WIDTH:f0], wl[:, f0 + ATT_HEADS:]], axis=1)
    wf = jnp.pad(wl[:, f0:f0 + ATT_HEADS], ((0, 0), (0, LANES - ATT_HEADS)))
    return w_main.astype(bf16), wf


def _cumsum_kernel(x_ref, kp_ref, ka_ref, carry_ref, *, tc):
    @pl.when(pl.program_id(1) == 0)
    def _():
        carry_ref[...] = jnp.zeros_like(carry_ref)

    row = lax.broadcasted_iota(jnp.int32, (tc, tc), 0)
    col = lax.broadcasted_iota(jnp.int32, (tc, tc), 1)
    tri = jnp.where(col <= row, 1.0, 0.0).astype(bf16)
    hi, mid, lo = _split3(x_ref[...])
    c = (jnp.dot(tri, hi, preferred_element_type=f32)
         + jnp.dot(tri, mid, preferred_element_type=f32)
         + jnp.dot(tri, lo, preferred_element_type=f32)) + carry_ref[...]
    carry_ref[...] = c[tc - 1:tc, :]
    hw = ATT_HEADS * LANES
    src = lax.broadcasted_iota(jnp.int32, (LANES, hw), 0)
    dst = lax.broadcasted_iota(jnp.int32, (LANES, hw), 1)
    fcols = jnp.zeros((tc, hw), f32)
    for p, part in enumerate(_split3(c * LOG2E)):
        place = jnp.where(dst == src * LANES + HEAD_DIM + p, 1.0, 0.0).astype(bf16)
        fcols = fcols + jnp.dot(part, place, preferred_element_type=f32)
    for hd in range(ATT_HEADS):
        ka_ref[0, hd] = (kp_ref[0, hd].astype(f32)
                         + fcols[:, hd * LANES:(hd + 1) * LANES]).astype(bf16)


def _cumsum(logf, kp, tc):
    B, H, S, _ = kp.shape
    nb = S // tc
    head_spec = pl.BlockSpec((1, H, tc, LANES), lambda b, j: (b, 0, j, 0))
    return pl.pallas_call(
        functools.partial(_cumsum_kernel, tc=tc),
        grid=(B, nb),
        in_specs=[pl.BlockSpec((tc, LANES), lambda b, j: (b * nb + j, 0)), head_spec],
        out_specs=head_spec,
        out_shape=jax.ShapeDtypeStruct(kp.shape, bf16),
        scratch_shapes=[pltpu.VMEM((1, LANES), f32)],
        compiler_params=_cparams(("arbitrary", "arbitrary")),
        name="cumsum_logf",
    )(logf, kp)


def _attn_kernel(jstart_ref, qa_ref, ka_ref, vt_ref, o_ref, acc_ref, m_ref, *, tq, hg):
    i = pl.program_id(2)
    m_ref[...] = jnp.full(m_ref.shape, NEG_BIG, f32)
    acc_ref[...] = jnp.zeros(acc_ref.shape, f32)

    def step(j, masked):
        start = pl.multiple_of(j * tq, tq)

        def scores(h):
            k = ka_ref[0, h, pl.ds(start, tq), :]
            s = lax.dot_general(k, qa_ref[0, h], _NT, preferred_element_type=f32)
            if masked:
                r = lax.broadcasted_iota(jnp.int32, (tq, tq), 0)
                c = lax.broadcasted_iota(jnp.int32, (tq, tq), 1)
                s = jnp.where(r <= c, s, NEG_BIG)
            return s

        def new_max(h, s):
            return jnp.maximum(m_ref[h], jnp.max(s, axis=0, keepdims=True))

        def weights(s, m_new):
            return jnp.exp2(s - m_new).astype(bf16)

        def accumulate(h, p, m_new):
            pv = jnp.dot(vt_ref[0, h, :, pl.ds(start, tq)], p, preferred_element_type=f32)
            acc_ref[h] = jnp.exp2(m_ref[h] - m_new) * acc_ref[h] + pv
            m_ref[h] = m_new

        s, mx, p = {}, {}, {}
        for t in range(hg + 3):
            if t < hg:
                s[t] = scores(t)
            if 0 <= t - 1 < hg:
                mx[t - 1] = new_max(t - 1, s[t - 1])
            if 0 <= t - 2 < hg:
                p[t - 2] = weights(s.pop(t - 2), mx[t - 2])
            if 0 <= t - 3 < hg:
                accumulate(t - 3, p.pop(t - 3), mx.pop(t - 3))

    def body(j, carry):
        step(j, False)
        return carry

    lax.fori_loop(jstart_ref[pl.program_id(0), pl.program_id(1), i], i, body, 0)
    step(i, True)
    for h in range(hg):
        o_ref[0, h * HEAD_DIM:(h + 1) * HEAD_DIM, :] = (
            acc_ref[h, :HEAD_DIM, :] / acc_ref[h, HEAD_DIM:HEAD_DIM + 1, :])


def _attention(jstart, qa, ka, vt, tq, hg):
    B, H, S, _ = qa.shape
    return pl.pallas_call(
        functools.partial(_attn_kernel, tq=tq, hg=hg),
        grid_spec=pltpu.PrefetchScalarGridSpec(
            num_scalar_prefetch=1,
            grid=(B, H // hg, S // tq),
            in_specs=[pl.BlockSpec((1, hg, tq, LANES), lambda b, g, i, js: (b, g, i, 0)),
                      pl.BlockSpec((1, hg, S, LANES), lambda b, g, i, js: (b, g, 0, 0),
                                   pipeline_mode=pl.Buffered(1)),
                      pl.BlockSpec((1, hg, V_ROWS, S), lambda b, g, i, js: (b, g, 0, 0),
                                   pipeline_mode=pl.Buffered(1))],
            out_specs=pl.BlockSpec((1, hg * HEAD_DIM, tq), lambda b, g, i, js: (b, g, i)),
            scratch_shapes=[pltpu.VMEM((hg, V_ROWS, tq), f32), pltpu.VMEM((hg, 1, tq), f32)]),
        out_shape=jax.ShapeDtypeStruct((B, H * HEAD_DIM, S), f32),
        compiler_params=_cparams(("parallel", "parallel", "arbitrary")),
        name="fox_attention",
    )(jstart, qa, ka, vt)


def _first_key_block(nrm, ka, tq, hg):
    B, H, S, _ = ka.shape
    nq = S // tq
    n2 = jnp.max(nrm.reshape(B, nq, tq, LANES), axis=2)
    qmax = jnp.sqrt(n2[:, :, :H]).transpose(0, 2, 1) * NORM_SLACK
    kmax = jnp.sqrt(n2[:, :, H:2 * H]).transpose(0, 2, 1) * NORM_SLACK
    blk = jnp.arange(nq, dtype=jnp.int32)
    earlier = blk[None, :] <= blk[:, None]
    kcum = jnp.max(jnp.where(earlier[None, None], kmax[:, :, None, :], 0.0), axis=3)
    fl = ka[:, :, :, HEAD_DIM:HEAD_DIM + F_PARTS].astype(f32).sum(-1)
    f_end = fl[:, :, tq - 1::tq]
    f_start = fl[:, :, ::tq]
    upper = qmax[:, :, :, None] * kcum[:, :, None, :] - f_end[:, :, None, :]
    lower = -qmax * kmax - f_start
    skip = upper < lower[:, :, :, None] - UNDERFLOW_LOG2
    lead = jnp.min(jnp.where(skip, nq, blk[None, None, None, :]), axis=3)
    lead = jnp.minimum(lead, blk[None, None, :])
    return lead.reshape(B, H // hg, hg, nq).min(axis=2).astype(jnp.int32)


def _mixout_kernel(oatt_ref, gm_ref, cv_ref, cvp_ref, x_ref, sgug_ref, wsp_ref, bsp_ref,
                   cw_ref, mg_ref, wout_ref, n2g_ref, xo_ref, h2_ref, *, tb, blocks_per_seq):
    i = pl.program_id(0)
    gm = gm_ref[...]
    u = _gelu(gm[:, :GMLP_WIDTH])
    v = _gelu(gm[:, GMLP_WIDTH:])
    gr = lax.broadcasted_iota(jnp.int32, (GMLP_WIDTH, GMLP_WIDTH), 0) // GROUP_DIM
    gc = lax.broadcasted_iota(jnp.int32, (GMLP_WIDTH, GMLP_WIDTH), 1) // GROUP_DIM
    grp = jnp.where(gr == gc, 1.0 / GROUP_DIM, 0.0).astype(bf16)
    vh, vm, vl = _split3(v * v)
    ms = (jnp.dot(vh, grp, preferred_element_type=f32)
          + jnp.dot(vm, grp, preferred_element_type=f32)
          + jnp.dot(vl, grp, preferred_element_type=f32))
    vn = (v * lax.rsqrt(ms + EPS) * sgug_ref[...]).astype(bf16)
    tr = lax.broadcasted_iota(jnp.int32, (CHUNK, CHUNK), 0)
    tc = lax.broadcasted_iota(jnp.int32, (CHUNK, CHUNK), 1)
    lane_g = lax.broadcasted_iota(jnp.int32, (CHUNK, GMLP_WIDTH), 1) // GROUP_DIM
    mixed_chunks = []
    for c in range(tb // CHUNK):
        vc = vn[c * CHUNK:(c + 1) * CHUNK, :]
        mixed = jnp.zeros((CHUNK, GMLP_WIDTH), f32)
        for g in range(GMLP_GROUPS):
            wg = jnp.where(tc <= tr, wsp_ref[g], 0.0).astype(bf16)
            mg = jnp.dot(wg, vc, preferred_element_type=f32)
            mixed = jnp.where(lane_g == g, mg, mixed)
        mixed_chunks.append(mixed + bsp_ref[...])
    o_gmlp = u * jnp.concatenate(mixed_chunks, axis=0)
    cv = cv_ref[...]
    c_in = cv[:, :CONV_WIDTH]
    c_b = cv[:, CONV_WIDTH:2 * CONV_WIDTH]
    c_c = cv[:, 2 * CONV_WIDTH:]
    z = c_c * c_in
    cvp = cvp_ref[...]
    zp = cvp[:, 2 * CONV_WIDTH:] * cvp[:, :CONV_WIDTH]
    zp = jnp.where(i % blocks_per_seq == 0, jnp.zeros_like(zp), zp)
    rowi = lax.broadcasted_iota(jnp.int32, (tb, CONV_WIDTH), 0)
    z1 = pltpu.roll(z, 1, axis=0)
    z1 = jnp.where(rowi == 0, zp[7:8, :], z1)
    z2 = pltpu.roll(z, 2, axis=0)
    z2 = jnp.where(rowi == 0, zp[6:7, :], z2)
    z2 = jnp.where(rowi == 1, zp[7:8, :], z2)
    cw = cw_ref[...]
    o_conv = c_b * (cw[0:1, :] * z2 + cw[1:2, :] * z1 + cw[2:3, :] * z)
    mg_ = mg_ref[...]
    o_att = oatt_ref[0].T
    mixed_all = jnp.concatenate([
        _rms(o_att, mg_[:, :ATT_WIDTH]),
        _rms(o_gmlp, mg_[:, ATT_WIDTH:ATT_WIDTH + GMLP_WIDTH]),
        _rms(o_conv, mg_[:, ATT_WIDTH + GMLP_WIDTH:])], axis=-1).astype(bf16)
    xn = x_ref[...] + jnp.dot(mixed_all, wout_ref[...], preferred_element_type=f32)
    xo_ref[...] = xn
    h2_ref[...] = _rms(xn, n2g_ref[...]).T.astype(bf16)


def _mix_out(oatt_t, gm, cv, x2, sgug, wsp, bspf, cw, mg, wout, n2g, S, tb):
    T = x2.shape[0]
    bps = S // tb
    const2 = lambda i: (0, 0)
    row = lambda i: (i, 0)
    return pl.pallas_call(
        functools.partial(_mixout_kernel, tb=tb, blocks_per_seq=bps),
        grid=(T // tb,),
        in_specs=[pl.BlockSpec((1, ATT_WIDTH, tb), lambda i: (i // bps, 0, i % bps)),
                  pl.BlockSpec((tb, 2 * GMLP_WIDTH), row),
                  pl.BlockSpec((tb, 3 * CONV_WIDTH), row),
                  pl.BlockSpec((8, 3 * CONV_WIDTH),
                               lambda i: (jnp.maximum(i * (tb // 8) - 1, 0), 0)),
                  pl.BlockSpec((tb, D_MODEL), row),
                  pl.BlockSpec((1, GMLP_WIDTH), const2),
                  pl.BlockSpec((GMLP_GROUPS, CHUNK, CHUNK), lambda i: (0, 0, 0)),
                  pl.BlockSpec((CHUNK, GMLP_WIDTH), const2),
                  pl.BlockSpec((8, CONV_WIDTH), const2),
                  pl.BlockSpec((1, MIX_WIDTH), const2),
                  pl.BlockSpec((MIX_WIDTH, D_MODEL), const2),
                  pl.BlockSpec((1, D_MODEL), const2)],
        out_specs=[pl.BlockSpec((tb, D_MODEL), row),
                   pl.BlockSpec((D_MODEL, tb), lambda i: (0, i))],
        out_shape=[jax.ShapeDtypeStruct((T, D_MODEL), f32),
                   jax.ShapeDtypeStruct((D_MODEL, T), bf16)],
        compiler_params=_cparams(("parallel",)),
        name="mix_out",
    )(oatt_t, gm, cv, cv, x2, sgug, wsp, bspf, cw, mg, wout, n2g)


def _colmax(x):
    return jnp.max(x, axis=0, keepdims=True)


def _colsum(x):
    return jnp.sum(x, axis=0, keepdims=True)


def _tree(op, xs):
    while len(xs) > 1:
        xs = [op(xs[i], xs[i + 1]) if i + 1 < len(xs) else xs[i] for i in range(0, len(xs), 2)]
    return xs[0]


def _top16(s, vals_ref):
    for a in range(PEER_TOPK):
        m = _colmax(s)
        s = jnp.where(s == m, RANK_MARK * (1.0 + a * RANK_STEP), s)
        vals_ref[a:a + 1, :] = m
    rank = jnp.round((s - RANK_MARK) * (1.0 / (RANK_MARK * RANK_STEP)))
    return jnp.where(s <= RANK_MARK, rank, float(PEER_TOPK))


def _select_tables(s1, s2, v1_ref, v2_ref):
    r1 = _top16(s1, v1_ref)
    r2 = _top16(s2, v2_ref)
    v1 = v1_ref[...]
    v2 = v2_ref[...]
    cands = [v1[0:1] + v2[0:8]]
    cands += [v1[a:a + 1] + v2[0:8] for a in range(1, 8)]
    cands += [v1[0:1] + v2[8:16], v1[8:16] + v2[0:1]]
    cmax = v1[0:1] + v2[0:1]
    work = list(cands)
    tau = cmax
    cnt = jnp.zeros_like(cmax)
    for _ in range(PEER_TOPK):
        m = _colmax(_tree(jnp.maximum, work))
        tau = jnp.where(cnt < float(PEER_TOPK), m, tau)
        hits = [w == m for w in work]
        cnt = cnt + _colsum(_tree(jnp.add, [jnp.where(h, 1.0, 0.0) for h in hits]))
        work = [jnp.where(h, -jnp.inf, w) for h, w in zip(hits, work)]
    sel = [c >= tau for c in cands]
    zsum = _colsum(_tree(jnp.add, [jnp.where(s_, jnp.exp(c - cmax), 0.0)
                                   for s_, c in zip(sel, cands)]))
    ones = [jnp.where(s_, 1.0, 0.0) for s_ in sel]
    n_lo = [_colsum(ones[0] + ones[8])] + [_colsum(o) for o in ones[1:8]]
    n_hi = ones[9]
    n1 = jnp.zeros(r1.shape, f32)
    for a in range(8):
        n1 = jnp.where(r1 == float(a), n_lo[a], n1)
    for a in range(8, PEER_TOPK):
        n1 = jnp.where(r1 == float(a), n_hi[a - 8:a - 7], n1)
    e2 = jnp.exp(s2 - v2[0:1])
    c1 = jnp.exp(s1 - v1[0:1]) / zsum
    return r2.astype(bf16), e2.astype(bf16), n1, c1


def _peerprep_kernel(h2_ref, wq_ref, keys_ref, r2_ref, e2_ref, n1_ref, c1_ref,
                     v1_ref, v2_ref):
    qT = jnp.dot(wq_ref[...], h2_ref[...], preferred_element_type=f32)
    qb = qT.astype(bf16)
    s1 = jnp.dot(keys_ref[0], qb[:D_HALF], preferred_element_type=f32)
    s2 = jnp.dot(keys_ref[1], qb[D_HALF:], preferred_element_type=f32)
    lpc = SLAB // LANES
    for c in range(r2_ref.shape[1]):
        r2s, e2s = [], []
        for k in range(lpc):
            t = c * lpc + k
            ls = slice(t * LANES, (t + 1) * LANES)
            r2, e2, n1, c1 = _select_tables(s1[:, ls], s2[:, ls], v1_ref, v2_ref)
            r2s.append(r2)
            e2s.append(e2)
            n1_ref[0, t] = n1
            c1_ref[0, t] = c1
        r2_ref[0, c] = jnp.concatenate(r2s, axis=1)
        e2_ref[0, c] = jnp.concatenate(e2s, axis=1)


def _peer_prep(h2t, wqT, keys, tb):
    T = h2t.shape[1]
    spec = pl.BlockSpec((1, tb // SLAB, N_KEYS, SLAB), lambda t, h: (h, t, 0, 0))
    shp = (PEER_HEADS, T // SLAB, N_KEYS, SLAB)
    rspec = pl.BlockSpec((1, tb // LANES, N_KEYS, LANES), lambda t, h: (h, t, 0, 0))
    rshp = (PEER_HEADS, T // LANES, N_KEYS, LANES)
    return pl.pallas_call(
        _peerprep_kernel,
        grid=(T // tb, PEER_HEADS),
        in_specs=[pl.BlockSpec((D_MODEL, tb), lambda t, h: (0, t)),
                  pl.BlockSpec((D_KEY, D_MODEL), lambda t, h: (h, 0)),
                  pl.BlockSpec((2, N_KEYS, D_HALF), lambda t, h: (0, 0, 0))],
        out_specs=[spec, spec, rspec, rspec],
        out_shape=[jax.ShapeDtypeStruct(shp, bf16), jax.ShapeDtypeStruct(shp, bf16),
                   jax.ShapeDtypeStruct(rshp, f32), jax.ShapeDtypeStruct(rshp, f32)],
        scratch_shapes=[pltpu.VMEM((PEER_TOPK, LANES), f32), pltpu.VMEM((PEER_TOPK, LANES), f32)],
        compiler_params=_cparams(("parallel", "arbitrary")),
        name="peer_prep",
    )(h2t, wqT, keys)


def _peermain_kernel(h2_ref, u_ref, vt_ref, r2_ref, e2_ref, n1_ref, c1_ref, x_ref,
                     o_ref, acc_ref, w_ref, *, rows):
    e = pl.program_id(1)
    tb = h2_ref.shape[1]
    nc = tb // SLAB
    lpc = SLAB // LANES

    @pl.when(e == 0)
    def _():
        acc_ref[...] = jnp.zeros_like(acc_ref)

    base = e * rows

    def row_tile(ref, h, r, c):
        row = jnp.concatenate(
            [jnp.broadcast_to(ref[h, c * lpc + k, pl.ds(base + r, 1), :], (BF16_ROWS, LANES))
             for k in range(lpc)], axis=1).astype(bf16)
        return jnp.tile(row, (N_KEYS // BF16_ROWS, 1))

    zero = jnp.zeros((N_KEYS, SLAB), bf16)

    def gate_slab(c, carry):
        for r in range(rows):
            g = None
            for h in range(PEER_HEADS):
                term = jnp.where(r2_ref[h, c] < row_tile(n1_ref, h, r, c), e2_ref[h, c], zero)
                term = term * row_tile(c1_ref, h, r, c)
                g = term if g is None else g + term
            w_ref[c, r * N_KEYS:(r + 1) * N_KEYS, :] = g
        return carry

    lax.fori_loop(0, nc, gate_slab, 0)
    gates = jnp.concatenate([w_ref[c] for c in range(nc)], axis=1)
    aT = jnp.dot(u_ref[...], h2_ref[...], preferred_element_type=f32)
    w = _gelu(aT.astype(bf16)) * gates
    acc_ref[...] += jnp.dot(vt_ref[...], w, preferred_element_type=f32)

    @pl.when(e == pl.num_programs(1) - 1)
    def _():
        o_ref[...] = x_ref[...] + acc_ref[...].T


def _peer_main(h2t, u, vt, r2, e2, n1, c1, x2, tb, rows):
    T = h2t.shape[1]
    eb = rows * N_KEYS
    sel = pl.BlockSpec((PEER_HEADS, tb // SLAB, N_KEYS, SLAB), lambda t, e: (0, t, 0, 0))
    rsel = pl.BlockSpec((PEER_HEADS, tb // LANES, N_KEYS, LANES), lambda t, e: (0, t, 0, 0))
    return pl.pallas_call(
        functools.partial(_peermain_kernel, rows=rows),
        grid=(T // tb, N_KEYS // rows),
        in_specs=[pl.BlockSpec((D_MODEL, tb), lambda t, e: (0, t)),
                  pl.BlockSpec((eb, D_MODEL), lambda t, e: (e, 0)),
                  pl.BlockSpec((D_MODEL, eb), lambda t, e: (0, e)),
                  sel, sel, rsel, rsel,
                  pl.BlockSpec((tb, D_MODEL), lambda t, e: (t, 0))],
        out_specs=pl.BlockSpec((tb, D_MODEL), lambda t, e: (t, 0)),
        out_shape=jax.ShapeDtypeStruct((T, D_MODEL), f32),
        scratch_shapes=[pltpu.VMEM((D_MODEL, tb), f32),
                        pltpu.VMEM((tb // SLAB, eb, SLAB), bf16)],
        compiler_params=_cparams(("parallel", "arbitrary")),
        name="peer_main",
    )(h2t, u, vt, r2, e2, n1, c1, x2)


def _finalnorm_kernel(x_ref, g_ref, o_ref):
    o_ref[...] = _rms(x_ref[...], g_ref[...])


def _final_norm(x2, g, tb):
    T = x2.shape[0]
    return pl.pallas_call(
        _finalnorm_kernel,
        grid=(T // tb,),
        in_specs=[pl.BlockSpec((tb, D_MODEL), lambda i: (i, 0)),
                  pl.BlockSpec((1, D_MODEL), lambda i: (0, 0))],
        out_specs=pl.BlockSpec((tb, D_MODEL), lambda i: (i, 0)),
        out_shape=jax.ShapeDtypeStruct(x2.shape, f32),
        compiler_params=_cparams(("parallel",)),
        name="final_norm",
    )(x2, g)


def _block_sizes(S):
    return dict(tb_in=min(512, S), tc=min(512, S), tq=min(512, S), heads_per_step=8,
                tb_mix=min(256, S), tb_prep=min(512, S), tb_peer=min(512, S), rows=16)


def kernel(x, norm1_g, w_in, b_forget, sgu_norm_g, w_spatial, b_spatial, conv_w,
           mix_norm_g, w_out, norm2_g, w_query, sub_keys, expert_u, expert_v, final_g):
    B, S, D = x.shape
    T = B * S
    depth = w_in.shape[0]
    bs = _block_sizes(S)
    x2 = x.reshape(T, D)
    for l in range(depth):
        w_main, wf = _in_proj_weights(w_in[l])
        bfp = jnp.pad(b_forget[l], (0, LANES - ATT_HEADS)).reshape(1, LANES)
        qa, kp, vt, gm, cv, logf, nrm = _in_proj(x2, norm1_g[l].reshape(1, D), w_main, wf, bfp,
                                                 B, S, bs["tb_in"])
        ka = _cumsum(logf, kp, bs["tc"])
        jstart = _first_key_block(nrm, ka, bs["tq"], bs["heads_per_step"])
        oatt_t = _attention(jstart, qa, ka, vt, bs["tq"], bs["heads_per_step"])
        bspf = jnp.repeat(b_spatial[l].T, GROUP_DIM, axis=1)
        cw = jnp.pad(conv_w[l], ((0, 8 - CONV_K), (0, 0)))
        x2, h2t = _mix_out(oatt_t, gm, cv, x2, sgu_norm_g[l].reshape(1, GMLP_WIDTH),
                          w_spatial[l], bspf, cw, mix_norm_g[l].reshape(1, MIX_WIDTH),
                          w_out[l].astype(bf16), norm2_g[l].reshape(1, D), S, bs["tb_mix"])
        r2, e2, n1, c1 = _peer_prep(h2t, w_query[l].T.astype(bf16), sub_keys[l].astype(bf16),
                                    bs["tb_prep"])
        x2 = _peer_main(h2t, expert_u[l].astype(bf16), expert_v[l].T.astype(bf16),
                        r2, e2, n1, c1, x2, bs["tb_peer"], bs["rows"])
    return _final_norm(x2, final_g.reshape(1, D), bs["tb_in"]).reshape(B, S, D)
```

```python
import functools
import math

import jax
import jax.numpy as jnp
from jax import lax
from jax.experimental import pallas as pl
from jax.experimental.pallas import tpu as pltpu

f32 = jnp.float32
bf16 = jnp.bfloat16

D_MODEL = 1024
ATT_WIDTH = 512
ATT_HEADS = 8
HEAD_DIM = 64
GMLP_WIDTH = 256
GMLP_GROUPS = 4
GROUP_DIM = 64
CHUNK = 128
CONV_WIDTH = 256
CONV_K = 3
MIX_WIDTH = 1024
PEER_HEADS = 8
N_KEYS = 128
PEER_TOPK = 16
D_KEY = 256
D_HALF = 128
EPS = 1e-6
LANES = 128
BF16_ROWS = 16
SLAB = 256
UNIT_ROWS = 8
NEG_BIG = -1e30
LOG2E = math.log2(math.e)
F_PARTS = 3
V_ROWS = 80
UNDERFLOW_LOG2 = 160.0
NORM_SLACK = 1.02
RANK_MARK = -3.0e38
RANK_STEP = 2.0 ** -10

VMEM_LIMIT = 56 * 1024 * 1024

_NT = (((1,), (1,)), ((), ()))


def _cparams(sem, flags=None):
    return pltpu.CompilerParams(dimension_semantics=sem, vmem_limit_bytes=VMEM_LIMIT, flags=flags)


def _gelu(x):
    z2 = (2.0 * 0.7978845608028654) * (x + 0.044715 * (x * x * x))
    return x / (1.0 + jnp.exp(-z2))


def _rms(x, g):
    return x * lax.rsqrt(jnp.mean(x * x, axis=-1, keepdims=True) + EPS) * g


def _split3(x):
    hi = x.astype(bf16)
    r = x - hi.astype(f32)
    mid = r.astype(bf16)
    lo = (r - mid.astype(f32)).astype(bf16)
    return hi, mid, lo


def _inproj_kernel(x_ref, g_ref, w_ref, wf_ref, bf_ref,
                   qa_ref, kp_ref, vt_ref, gm_ref, cv_ref, logf_ref, nrm_ref):
    tb = x_ref.shape[0]
    h = _rms(x_ref[...], g_ref[...])
    hb = h.astype(bf16)
    z = jnp.dot(hb, w_ref[...], preferred_element_type=f32)
    hw = ATT_HEADS * LANES
    lane = lax.broadcasted_iota(jnp.int32, (1, hw), 1) % LANES
    fcol = (lane >= HEAD_DIM) & (lane < HEAD_DIM + F_PARTS)
    zq = z[:, :hw] * (HEAD_DIM ** -0.5 * LOG2E)
    zk = z[:, hw:2 * hw]
    qa = jnp.where(fcol, -1.0, zq).astype(bf16)
    kp = zk.astype(bf16)
    vT = z[:, 2 * hw:2 * hw + ATT_WIDTH].T
    tail = jnp.where(lax.broadcasted_iota(jnp.int32, (V_ROWS - HEAD_DIM, tb), 0) == 0,
                     1.0, 0.0).astype(bf16)
    for hd in range(ATT_HEADS):
        qa_ref[0, hd] = qa[:, hd * LANES:(hd + 1) * LANES]
        kp_ref[0, hd] = kp[:, hd * LANES:(hd + 1) * LANES]
        vt_ref[0, hd, :HEAD_DIM, :] = vT[hd * HEAD_DIM:(hd + 1) * HEAD_DIM, :].astype(bf16)
        vt_ref[0, hd, HEAD_DIM:, :] = tail
    c0 = 2 * hw + ATT_WIDTH
    gm_ref[...] = z[:, c0:c0 + 2 * GMLP_WIDTH]
    cv_ref[...] = z[:, c0 + 2 * GMLP_WIDTH:]
    r = lax.broadcasted_iota(jnp.int32, (hw, LANES), 0) // LANES
    c = lax.broadcasted_iota(jnp.int32, (hw, LANES), 1)
    ind_q = jnp.where(r == c, 1.0, 0.0).astype(bf16)
    ind_k = jnp.where(r + ATT_HEADS == c, 1.0, 0.0).astype(bf16)
    nrm_ref[...] = (jnp.dot((zq * zq).astype(bf16), ind_q, preferred_element_type=f32)
                    + jnp.dot((zk * zk).astype(bf16), ind_k, preferred_element_type=f32))
    hl = (h - hb.astype(f32)).astype(bf16)
    wf = wf_ref[...]
    wfh = wf.astype(bf16)
    wfl = (wf - wfh.astype(f32)).astype(bf16)
    zf = (jnp.dot(hb, wfh, preferred_element_type=f32)
          + jnp.dot(hb, wfl, preferred_element_type=f32)
          + jnp.dot(hl, wfh, preferred_element_type=f32)) + bf_ref[...]
    logf_ref[...] = jnp.minimum(zf, 0.0) - jnp.log1p(jnp.exp(-jnp.abs(zf)))


def _in_proj(x2, g, w, wf, bfp, B, S, tb):
    T = x2.shape[0]
    bps = S // tb
    ncol = w.shape[1]
    const = lambda i: (0, 0)
    row = lambda i: (i, 0)
    head_spec = pl.BlockSpec((1, ATT_HEADS, tb, LANES), lambda i: (i // bps, 0, i % bps, 0))
    head_shape = jax.ShapeDtypeStruct((B, ATT_HEADS, S, LANES), bf16)
    return pl.pallas_call(
        _inproj_kernel,
        grid=(T // tb,),
        in_specs=[pl.BlockSpec((tb, D_MODEL), row),
                  pl.BlockSpec((1, D_MODEL), const),
                  pl.BlockSpec((D_MODEL, ncol), const),
                  pl.BlockSpec((D_MODEL, LANES), const),
                  pl.BlockSpec((1, LANES), const)],
        out_specs=[head_spec, head_spec,
                   pl.BlockSpec((1, ATT_HEADS, V_ROWS, tb), lambda i: (i // bps, 0, 0, i % bps)),
                   pl.BlockSpec((tb, 2 * GMLP_WIDTH), row),
                   pl.BlockSpec((tb, 3 * CONV_WIDTH), row),
                   pl.BlockSpec((tb, LANES), row),
                   pl.BlockSpec((tb, LANES), row)],
        out_shape=[head_shape, head_shape,
                   jax.ShapeDtypeStruct((B, ATT_HEADS, V_ROWS, S), bf16),
                   jax.ShapeDtypeStruct((T, 2 * GMLP_WIDTH), f32),
                   jax.ShapeDtypeStruct((T, 3 * CONV_WIDTH), f32),
                   jax.ShapeDtypeStruct((T, LANES), f32),
                   jax.ShapeDtypeStruct((T, LANES), f32)],
        compiler_params=_cparams(("parallel",)),
        name="in_proj",
    )(x2, g, w, wf, bfp)


def _in_proj_weights(wl):
    f0 = 3 * ATT_WIDTH

    def per_head(cols):
        w3 = cols.reshape(D_MODEL, ATT_HEADS, HEAD_DIM)
        return jnp.pad(w3, ((0, 0), (0, 0), (0, LANES - HEAD_DIM))).reshape(D_MODEL, -1)

    w_main = jnp.concatenate([per_head(wl[:, :ATT_WIDTH]), per_head(wl[:, ATT_WIDTH:2 * ATT_WIDTH]),
                              wl[:, 2 * ATT_WIDTH:f0], wl[:, f0 + ATT_HEADS:]], axis=1)
    wf = jnp.pad(wl[:, f0:f0 + ATT_HEADS], ((0, 0), (0, LANES - ATT_HEADS)))
    return w_main.astype(bf16), wf


def _cumsum_kernel(x_ref, kp_ref, ka_ref, carry_ref, *, tc):
    @pl.when(pl.program_id(1) == 0)
    def _():
        carry_ref[...] = jnp.zeros_like(carry_ref)

    row = lax.broadcasted_iota(jnp.int32, (tc, tc), 0)
    col = lax.broadcasted_iota(jnp.int32, (tc, tc), 1)
    tri = jnp.where(col <= row, 1.0, 0.0).astype(bf16)
    hi, mid, lo = _split3(x_ref[...])
    c = (jnp.dot(tri, hi, preferred_element_type=f32)
         + jnp.dot(tri, mid, preferred_element_type=f32)
         + jnp.dot(tri, lo, preferred_element_type=f32)) + carry_ref[...]
    carry_ref[...] = c[tc - 1:tc, :]
    hw = ATT_HEADS * LANES
    src = lax.broadcasted_iota(jnp.int32, (LANES, hw), 0)
    dst = lax.broadcasted_iota(jnp.int32, (LANES, hw), 1)
    fcols = jnp.zeros((tc, hw), f32)
    for p, part in enumerate(_split3(c * LOG2E)):
        place = jnp.where(dst == src * LANES + HEAD_DIM + p, 1.0, 0.0).astype(bf16)
        fcols = fcols + jnp.dot(part, place, preferred_element_type=f32)
    for hd in range(ATT_HEADS):
        ka_ref[0, hd] = (kp_ref[0, hd].astype(f32)
                         + fcols[:, hd * LANES:(hd + 1) * LANES]).astype(bf16)


def _cumsum(logf, kp, tc):
    B, H, S, _ = kp.shape
    nb = S // tc
    head_spec = pl.BlockSpec((1, H, tc, LANES), lambda b, j: (b, 0, j, 0))
    return pl.pallas_call(
        functools.partial(_cumsum_kernel, tc=tc),
        grid=(B, nb),
        in_specs=[pl.BlockSpec((tc, LANES), lambda b, j: (b * nb + j, 0)), head_spec],
        out_specs=head_spec,
        out_shape=jax.ShapeDtypeStruct(kp.shape, bf16),
        scratch_shapes=[pltpu.VMEM((1, LANES), f32)],
        compiler_params=_cparams(("arbitrary", "arbitrary")),
        name="cumsum_logf",
    )(logf, kp)


def _attn_kernel(jstart_ref, qa_ref, ka_ref, vt_ref, o_ref, acc_ref, m_ref, *, tq, hg):
    i = pl.program_id(2)
    m_ref[...] = jnp.full(m_ref.shape, NEG_BIG, f32)
    acc_ref[...] = jnp.zeros(acc_ref.shape, f32)

    def step(j, masked):
        start = pl.multiple_of(j * tq, tq)

        def scores(h):
            k = ka_ref[0, h, pl.ds(start, tq), :]
            s = lax.dot_general(k, qa_ref[0, h], _NT, preferred_element_type=f32)
            if masked:
                r = lax.broadcasted_iota(jnp.int32, (tq, tq), 0)
                c = lax.broadcasted_iota(jnp.int32, (tq, tq), 1)
                s = jnp.where(r <= c, s, NEG_BIG)
            return s

        def new_max(h, s):
            return jnp.maximum(m_ref[h], jnp.max(s, axis=0, keepdims=True))

        def weights(s, m_new):
            return jnp.exp2(s - m_new).astype(bf16)

        def accumulate(h, p, m_new):
            pv = jnp.dot(vt_ref[0, h, :, pl.ds(start, tq)], p, preferred_element_type=f32)
            acc_ref[h] = jnp.exp2(m_ref[h] - m_new) * acc_ref[h] + pv
            m_ref[h] = m_new

        s, mx, p = {}, {}, {}
        for t in range(hg + 3):
            if t < hg:
                s[t] = scores(t)
            if 0 <= t - 1 < hg:
                mx[t - 1] = new_max(t - 1, s[t - 1])
            if 0 <= t - 2 < hg:
                p[t - 2] = weights(s.pop(t - 2), mx[t - 2])
            if 0 <= t - 3 < hg:
                accumulate(t - 3, p.pop(t - 3), mx.pop(t - 3))

    def body(j, carry):
        step(j, False)
        return carry

    lax.fori_loop(jstart_ref[pl.program_id(0), pl.program_id(1), i], i, body, 0)
    step(i, True)
    for h in range(hg):
        o_ref[0, h * HEAD_DIM:(h + 1) * HEAD_DIM, :] = (
            acc_ref[h, :HEAD_DIM, :] / acc_ref[h, HEAD_DIM:HEAD_DIM + 1, :])


def _attention(jstart, qa, ka, vt, tq, hg):
    B, H, S, _ = qa.shape
    return pl.pallas_call(
        functools.partial(_attn_kernel, tq=tq, hg=hg),
        grid_spec=pltpu.PrefetchScalarGridSpec(
            num_scalar_prefetch=1,
            grid=(B, H // hg, S // tq),
            in_specs=[pl.BlockSpec((1, hg, tq, LANES), lambda b, g, i, js: (b, g, i, 0)),
                      pl.BlockSpec((1, hg, S, LANES), lambda b, g, i, js: (b, g, 0, 0),
                                   pipeline_mode=pl.Buffered(1)),
                      pl.BlockSpec((1, hg, V_ROWS, S), lambda b, g, i, js: (b, g, 0, 0),
                                   pipeline_mode=pl.Buffered(1))],
            out_specs=pl.BlockSpec((1, hg * HEAD_DIM, tq), lambda b, g, i, js: (b, g, i)),
            scratch_shapes=[pltpu.VMEM((hg, V_ROWS, tq), f32), pltpu.VMEM((hg, 1, tq), f32)]),
        out_shape=jax.ShapeDtypeStruct((B, H * HEAD_DIM, S), f32),
        compiler_params=_cparams(("parallel", "parallel", "arbitrary")),
        name="fox_attention",
    )(jstart, qa, ka, vt)


def _first_key_block(nrm, ka, tq, hg):
    B, H, S, _ = ka.shape
    nq = S // tq
    n2 = jnp.max(nrm.reshape(B, nq, tq, LANES), axis=2)
    qmax = jnp.sqrt(n2[:, :, :H]).transpose(0, 2, 1) * NORM_SLACK
    kmax = jnp.sqrt(n2[:, :, H:2 * H]).transpose(0, 2, 1) * NORM_SLACK
    blk = jnp.arange(nq, dtype=jnp.int32)
    earlier = blk[None, :] <= blk[:, None]
    kcum = jnp.max(jnp.where(earlier[None, None], kmax[:, :, None, :], 0.0), axis=3)
    fl = ka[:, :, :, HEAD_DIM:HEAD_DIM + F_PARTS].astype(f32).sum(-1)
    f_end = fl[:, :, tq - 1::tq]
    f_start = fl[:, :, ::tq]
    upper = qmax[:, :, :, None] * kcum[:, :, None, :] - f_end[:, :, None, :]
    lower = -qmax * kmax - f_start
    skip = upper < lower[:, :, :, None] - UNDERFLOW_LOG2
    lead = jnp.min(jnp.where(skip, nq, blk[None, None, None, :]), axis=3)
    lead = jnp.minimum(lead, blk[None, None, :])
    return lead.reshape(B, H // hg, hg, nq).min(axis=2).astype(jnp.int32)


def _mixout_kernel(oatt_ref, gm_ref, cv_ref, cvp_ref, x_ref, sgug_ref, wsp_ref, bsp_ref,
                   cw_ref, mg_ref, wout_ref, n2g_ref, xo_ref, h2_ref, *, tb, blocks_per_seq):
    i = pl.program_id(0)
    gm = gm_ref[...]
    u = _gelu(gm[:, :GMLP_WIDTH])
    v = _gelu(gm[:, GMLP_WIDTH:])
    gr = lax.broadcasted_iota(jnp.int32, (GMLP_WIDTH, GMLP_WIDTH), 0) // GROUP_DIM
    gc = lax.broadcasted_iota(jnp.int32, (GMLP_WIDTH, GMLP_WIDTH), 1) // GROUP_DIM
    grp = jnp.where(gr == gc, 1.0 / GROUP_DIM, 0.0).astype(bf16)
    vh, vm, vl = _split3(v * v)
    ms = (jnp.dot(vh, grp, preferred_element_type=f32)
          + jnp.dot(vm, grp, preferred_element_type=f32)
          + jnp.dot(vl, grp, preferred_element_type=f32))
    vn = (v * lax.rsqrt(ms + EPS) * sgug_ref[...]).astype(bf16)
    tr = lax.broadcasted_iota(jnp.int32, (CHUNK, CHUNK), 0)
    tc = lax.broadcasted_iota(jnp.int32, (CHUNK, CHUNK), 1)
    lane_g = lax.broadcasted_iota(jnp.int32, (CHUNK, GMLP_WIDTH), 1) // GROUP_DIM
    mixed_chunks = []
    for c in range(tb // CHUNK):
        vc = vn[c * CHUNK:(c + 1) * CHUNK, :]
        mixed = jnp.zeros((CHUNK, GMLP_WIDTH), f32)
        for g in range(GMLP_GROUPS):
            wg = jnp.where(tc <= tr, wsp_ref[g], 0.0).astype(bf16)
            mg = jnp.dot(wg, vc, preferred_element_type=f32)
            mixed = jnp.where(lane_g == g, mg, mixed)
        mixed_chunks.append(mixed + bsp_ref[...])
    o_gmlp = u * jnp.concatenate(mixed_chunks, axis=0)
    cv = cv_ref[...]
    c_in = cv[:, :CONV_WIDTH]
    c_b = cv[:, CONV_WIDTH:2 * CONV_WIDTH]
    c_c = cv[:, 2 * CONV_WIDTH:]
    z = c_c * c_in
    cvp = cvp_ref[...]
    zp = cvp[:, 2 * CONV_WIDTH:] * cvp[:, :CONV_WIDTH]
    zp = jnp.where(i % blocks_per_seq == 0, jnp.zeros_like(zp), zp)
    rowi = lax.broadcasted_iota(jnp.int32, (tb, CONV_WIDTH), 0)
    z1 = pltpu.roll(z, 1, axis=0)
    z1 = jnp.where(rowi == 0, zp[7:8, :], z1)
    z2 = pltpu.roll(z, 2, axis=0)
    z2 = jnp.where(rowi == 0, zp[6:7, :], z2)
    z2 = jnp.where(rowi == 1, zp[7:8, :], z2)
    cw = cw_ref[...]
    o_conv = c_b * (cw[0:1, :] * z2 + cw[1:2, :] * z1 + cw[2:3, :] * z)
    mg_ = mg_ref[...]
    o_att = oatt_ref[0].T
    mixed_all = jnp.concatenate([
        _rms(o_att, mg_[:, :ATT_WIDTH]),
        _rms(o_gmlp, mg_[:, ATT_WIDTH:ATT_WIDTH + GMLP_WIDTH]),
        _rms(o_conv, mg_[:, ATT_WIDTH + GMLP_WIDTH:])], axis=-1).astype(bf16)
    xn = x_ref[...] + jnp.dot(mixed_all, wout_ref[...], preferred_element_type=f32)
    xo_ref[...] = xn
    h2_ref[...] = _rms(xn, n2g_ref[...]).T.astype(bf16)


def _mix_out(oatt_t, gm, cv, x2, sgug, wsp, bspf, cw, mg, wout, n2g, S, tb):
    T = x2.shape[0]
    bps = S // tb
    const2 = lambda i: (0, 0)
    row = lambda i: (i, 0)
    return pl.pallas_call(
        functools.partial(_mixout_kernel, tb=tb, blocks_per_seq=bps),
        grid=(T // tb,),
        in_specs=[pl.BlockSpec((1, ATT_WIDTH, tb), lambda i: (i // bps, 0, i % bps)),
                  pl.BlockSpec((tb, 2 * GMLP_WIDTH), row),
                  pl.BlockSpec((tb, 3 * CONV_WIDTH), row),
                  pl.BlockSpec((8, 3 * CONV_WIDTH),
                               lambda i: (jnp.maximum(i * (tb // 8) - 1, 0), 0)),
                  pl.BlockSpec((tb, D_MODEL), row),
                  pl.BlockSpec((1, GMLP_WIDTH), const2),
                  pl.BlockSpec((GMLP_GROUPS, CHUNK, CHUNK), lambda i: (0, 0, 0)),
                  pl.BlockSpec((CHUNK, GMLP_WIDTH), const2),
                  pl.BlockSpec((8, CONV_WIDTH), const2),
                  pl.BlockSpec((1, MIX_WIDTH), const2),
                  pl.BlockSpec((MIX_WIDTH, D_MODEL), const2),
                  pl.BlockSpec((1, D_MODEL), const2)],
        out_specs=[pl.BlockSpec((tb, D_MODEL), row),
                   pl.BlockSpec((D_MODEL, tb), lambda i: (0, i))],
        out_shape=[jax.ShapeDtypeStruct((T, D_MODEL), f32),
                   jax.ShapeDtypeStruct((D_MODEL, T), bf16)],
        compiler_params=_cparams(("parallel",)),
        name="mix_out",
    )(oatt_t, gm, cv, cv, x2, sgug, wsp, bspf, cw, mg, wout, n2g)


def _colmax(x):
    return jnp.max(x, axis=0, keepdims=True)


def _colsum(x):
    return jnp.sum(x, axis=0, keepdims=True)


def _tree(op, xs):
    while len(xs) > 1:
        xs = [op(xs[i], xs[i + 1]) if i + 1 < len(xs) else xs[i] for i in range(0, len(xs), 2)]
    return xs[0]


def _top16(s, vals_ref):
    for a in range(PEER_TOPK):
        m = _colmax(s)
        s = jnp.where(s == m, RANK_MARK * (1.0 + a * RANK_STEP), s)
        vals_ref[a:a + 1, :] = m
    rank = jnp.round((s - RANK_MARK) * (1.0 / (RANK_MARK * RANK_STEP)))
    return jnp.where(s <= RANK_MARK, rank, float(PEER_TOPK))


def _select_tables(s1, s2, v1_ref, v2_ref):
    r1 = _top16(s1, v1_ref)
    r2 = _top16(s2, v2_ref)
    v1 = v1_ref[...]
    v2 = v2_ref[...]
    cands = [v1[0:1] + v2[0:8]]
    cands += [v1[a:a + 1] + v2[0:8] for a in range(1, 8)]
    cands += [v1[0:1] + v2[8:16], v1[8:16] + v2[0:1]]
    cmax = v1[0:1] + v2[0:1]
    work = list(cands)
    tau = cmax
    cnt = jnp.zeros_like(cmax)
    for _ in range(PEER_TOPK):
        m = _colmax(_tree(jnp.maximum, work))
        tau = jnp.where(cnt < float(PEER_TOPK), m, tau)
        hits = [w == m for w in work]
        cnt = cnt + _colsum(_tree(jnp.add, [jnp.where(h, 1.0, 0.0) for h in hits]))
        work = [jnp.where(h, -jnp.inf, w) for h, w in zip(hits, work)]
    sel = [c >= tau for c in cands]
    zsum = _colsum(_tree(jnp.add, [jnp.where(s_, jnp.exp(c - cmax), 0.0)
                                   for s_, c in zip(sel, cands)]))
    ones = [jnp.where(s_, 1.0, 0.0) for s_ in sel]
    n_lo = [_colsum(ones[0] + ones[8])] + [_colsum(o) for o in ones[1:8]]
    n_hi = ones[9]
    n1 = jnp.zeros(r1.shape, f32)
    for a in range(8):
        n1 = jnp.where(r1 == float(a), n_lo[a], n1)
    for a in range(8, PEER_TOPK):
        n1 = jnp.where(r1 == float(a), n_hi[a - 8:a - 7], n1)
    e2 = jnp.exp(s2 - v2[0:1])
    c1 = jnp.exp(s1 - v1[0:1]) / zsum
    return r2.astype(bf16), e2.astype(bf16), n1, c1


def _peerprep_kernel(h2_ref, wq_ref, keys_ref, r2_ref, e2_ref, n1_ref, c1_ref,
                     v1_ref, v2_ref):
    qT = jnp.dot(wq_ref[...], h2_ref[...], preferred_element_type=f32)
    qb = qT.astype(bf16)
    s1 = jnp.dot(keys_ref[0], qb[:D_HALF], preferred_element_type=f32)
    s2 = jnp.dot(keys_ref[1], qb[D_HALF:], preferred_element_type=f32)
    lpc = SLAB // LANES
    for c in range(r2_ref.shape[1]):
        r2s, e2s = [], []
        for k in range(lpc):
            t = c * lpc + k
            ls = slice(t * LANES, (t + 1) * LANES)
            r2, e2, n1, c1 = _select_tables(s1[:, ls], s2[:, ls], v1_ref, v2_ref)
            r2s.append(r2)
            e2s.append(e2)
            n1_ref[0, t] = n1
            c1_ref[0, t] = c1
        r2_ref[0, c] = jnp.concatenate(r2s, axis=1)
        e2_ref[0, c] = jnp.concatenate(e2s, axis=1)


def _peer_prep(h2t, wqT, keys, tb):
    T = h2t.shape[1]
    spec = pl.BlockSpec((1, tb // SLAB, N_KEYS, SLAB), lambda t, h: (h, t, 0, 0))
    shp = (PEER_HEADS, T // SLAB, N_KEYS, SLAB)
    rspec = pl.BlockSpec((1, tb // LANES, N_KEYS, LANES), lambda t, h: (h, t, 0, 0))
    rshp = (PEER_HEADS, T // LANES, N_KEYS, LANES)
    return pl.pallas_call(
        _peerprep_kernel,
        grid=(T // tb, PEER_HEADS),
        in_specs=[pl.BlockSpec((D_MODEL, tb), lambda t, h: (0, t)),
                  pl.BlockSpec((D_KEY, D_MODEL), lambda t, h: (h, 0)),
                  pl.BlockSpec((2, N_KEYS, D_HALF), lambda t, h: (0, 0, 0))],
        out_specs=[spec, spec, rspec, rspec],
        out_shape=[jax.ShapeDtypeStruct(shp, bf16), jax.ShapeDtypeStruct(shp, bf16),
                   jax.ShapeDtypeStruct(rshp, f32), jax.ShapeDtypeStruct(rshp, f32)],
        scratch_shapes=[pltpu.VMEM((PEER_TOPK, LANES), f32), pltpu.VMEM((PEER_TOPK, LANES), f32)],
        compiler_params=_cparams(("parallel", "arbitrary")),
        name="peer_prep",
    )(h2t, wqT, keys)


def _peermain_kernel(h2_ref, u_ref, vt_ref, r2_ref, e2_ref, n1_ref, c1_ref, x_ref,
                     o_ref, acc_ref, *, rows):
    e = pl.program_id(1)
    tb = h2_ref.shape[1]
    nc = tb // SLAB
    lpc = SLAB // LANES

    @pl.when(e == 0)
    def _():
        acc_ref[...] = jnp.zeros_like(acc_ref)

    base = e * rows

    def row_tile(ref, h, r, c):
        row = jnp.concatenate(
            [jnp.broadcast_to(ref[h, c * lpc + k, pl.ds(base + r, 1), :], (BF16_ROWS, LANES))
             for k in range(lpc)], axis=1).astype(bf16)
        return jnp.tile(row, (N_KEYS // BF16_ROWS, 1))

    zero = jnp.zeros((N_KEYS, SLAB), bf16)

    ub = UNIT_ROWS * N_KEYS
    units = [(c, u) for c in range(nc) for u in range(rows // UNIT_ROWS)]

    def activations(unit):
        c, u = unit
        return jnp.dot(u_ref[u * ub:(u + 1) * ub, :], h2_ref[:, c * SLAB:(c + 1) * SLAB],
                       preferred_element_type=f32)

    def gates(unit):
        c, u = unit
        out = []
        for r in range(u * UNIT_ROWS, (u + 1) * UNIT_ROWS):
            g = None
            for h in range(PEER_HEADS):
                term = jnp.where(r2_ref[h, c] < row_tile(n1_ref, h, r, c), e2_ref[h, c], zero)
                term = term * row_tile(c1_ref, h, r, c)
                g = term if g is None else g + term
            out.append(g)
        return jnp.concatenate(out, axis=0)

    def accumulate(unit, w):
        c, u = unit
        cs = slice(c * SLAB, (c + 1) * SLAB)
        acc_ref[:, cs] += jnp.dot(vt_ref[:, u * ub:(u + 1) * ub], w, preferred_element_type=f32)

    a, g, w = {}, {}, {}
    for t in range(len(units) + 3):
        if t < len(units):
            a[t] = activations(units[t])
        if 0 <= t - 1 < len(units):
            g[t - 1] = gates(units[t - 1])
        if 0 <= t - 2 < len(units):
            w[t - 2] = _gelu(a.pop(t - 2).astype(bf16)) * g.pop(t - 2)
        if 0 <= t - 3 < len(units):
            accumulate(units[t - 3], w.pop(t - 3))

    @pl.when(e == pl.num_programs(1) - 1)
    def _():
        o_ref[...] = x_ref[...] + acc_ref[...].T


def _peer_main(h2t, u, vt, r2, e2, n1, c1, x2, tb, rows):
    T = h2t.shape[1]
    eb = rows * N_KEYS
    sel = pl.BlockSpec((PEER_HEADS, tb // SLAB, N_KEYS, SLAB), lambda t, e: (0, t, 0, 0))
    rsel = pl.BlockSpec((PEER_HEADS, tb // LANES, N_KEYS, LANES), lambda t, e: (0, t, 0, 0))
    return pl.pallas_call(
        functools.partial(_peermain_kernel, rows=rows),
        grid=(T // tb, N_KEYS // rows),
        in_specs=[pl.BlockSpec((D_MODEL, tb), lambda t, e: (0, t)),
                  pl.BlockSpec((eb, D_MODEL), lambda t, e: (e, 0)),
                  pl.BlockSpec((D_MODEL, eb), lambda t, e: (0, e)),
                  sel, sel, rsel, rsel,
                  pl.BlockSpec((tb, D_MODEL), lambda t, e: (t, 0))],
        out_specs=pl.BlockSpec((tb, D_MODEL), lambda t, e: (t, 0)),
        out_shape=jax.ShapeDtypeStruct((T, D_MODEL), f32),
        scratch_shapes=[pltpu.VMEM((D_MODEL, tb), f32)],
        compiler_params=_cparams(("parallel", "arbitrary")),
        name="peer_main",
    )(h2t, u, vt, r2, e2, n1, c1, x2)


def _finalnorm_kernel(x_ref, g_ref, o_ref):
    o_ref[...] = _rms(x_ref[...], g_ref[...])


def _final_norm(x2, g, tb):
    T = x2.shape[0]
    return pl.pallas_call(
        _finalnorm_kernel,
        grid=(T // tb,),
        in_specs=[pl.BlockSpec((tb, D_MODEL), lambda i: (i, 0)),
                  pl.BlockSpec((1, D_MODEL), lambda i: (0, 0))],
        out_specs=pl.BlockSpec((tb, D_MODEL), lambda i: (i, 0)),
        out_shape=jax.ShapeDtypeStruct(x2.shape, f32),
        compiler_params=_cparams(("parallel",)),
        name="final_norm",
    )(x2, g)


def _block_sizes(S):
    return dict(tb_in=min(512, S), tc=min(512, S), tq=min(512, S), heads_per_step=8,
                tb_mix=min(256, S), tb_prep=min(512, S), tb_peer=min(512, S), rows=16)


def kernel(x, norm1_g, w_in, b_forget, sgu_norm_g, w_spatial, b_spatial, conv_w,
           mix_norm_g, w_out, norm2_g, w_query, sub_keys, expert_u, expert_v, final_g):
    B, S, D = x.shape
    T = B * S
    depth = w_in.shape[0]
    bs = _block_sizes(S)
    x2 = x.reshape(T, D)
    for l in range(depth):
        w_main, wf = _in_proj_weights(w_in[l])
        bfp = jnp.pad(b_forget[l], (0, LANES - ATT_HEADS)).reshape(1, LANES)
        qa, kp, vt, gm, cv, logf, nrm = _in_proj(x2, norm1_g[l].reshape(1, D), w_main, wf, bfp,
                                                 B, S, bs["tb_in"])
        ka = _cumsum(logf, kp, bs["tc"])
        jstart = _first_key_block(nrm, ka, bs["tq"], bs["heads_per_step"])
        oatt_t = _attention(jstart, qa, ka, vt, bs["tq"], bs["heads_per_step"])
        bspf = jnp.repeat(b_spatial[l].T, GROUP_DIM, axis=1)
        cw = jnp.pad(conv_w[l], ((0, 8 - CONV_K), (0, 0)))
        x2, h2t = _mix_out(oatt_t, gm, cv, x2, sgu_norm_g[l].reshape(1, GMLP_WIDTH),
                          w_spatial[l], bspf, cw, mix_norm_g[l].reshape(1, MIX_WIDTH),
                          w_out[l].astype(bf16), norm2_g[l].reshape(1, D), S, bs["tb_mix"])
        r2, e2, n1, c1 = _peer_prep(h2t, w_query[l].T.astype(bf16), sub_keys[l].astype(bf16),
                                    bs["tb_prep"])
        x2 = _peer_main(h2t, expert_u[l].astype(bf16), expert_v[l].T.astype(bf16),
                        r2, e2, n1, c1, x2, bs["tb_peer"], bs["rows"])
    return _final_norm(x2, final_g.reshape(1, D), bs["tb_in"]).reshape(B, S, D)
```

```python
import functools
import math

import jax
import jax.numpy as jnp
from jax import lax
from jax.experimental import pallas as pl
from jax.experimental.pallas import tpu as pltpu

f32 = jnp.float32
bf16 = jnp.bfloat16

D_MODEL = 1024
ATT_WIDTH = 512
ATT_HEADS = 8
HEAD_DIM = 64
GMLP_WIDTH = 256
GMLP_GROUPS = 4
GROUP_DIM = 64
CHUNK = 128
CONV_WIDTH = 256
CONV_K = 3
MIX_WIDTH = 1024
PEER_HEADS = 8
N_KEYS = 128
PEER_TOPK = 16
D_KEY = 256
D_HALF = 128
EPS = 1e-6
LANES = 128
BF16_ROWS = 16
SLAB = 256
UNIT_ROWS = 8
NEG_BIG = -1e30
LOG2E = math.log2(math.e)
GELU_K0 = -2.0 * math.sqrt(2.0 / math.pi)
GELU_K1 = GELU_K0 * 0.044715
F_PARTS = 3
V_ROWS = 80
UNDERFLOW_LOG2 = 160.0
NORM_SLACK = 1.02
RANK_MARK = -3.0e38
RANK_STEP = 2.0 ** -10

VMEM_LIMIT = 56 * 1024 * 1024

_NT = (((1,), (1,)), ((), ()))


def _cparams(sem, flags=None):
    return pltpu.CompilerParams(dimension_semantics=sem, vmem_limit_bytes=VMEM_LIMIT, flags=flags)


def _gelu(x):
    return x / (1.0 + jnp.exp(x * (GELU_K0 + GELU_K1 * (x * x))))


def _rms(x, g):
    return x * lax.rsqrt(jnp.mean(x * x, axis=-1, keepdims=True) + EPS) * g


def _split3(x):
    hi = x.astype(bf16)
    r = x - hi.astype(f32)
    mid = r.astype(bf16)
    lo = (r - mid.astype(f32)).astype(bf16)
    return hi, mid, lo


def _inproj_kernel(x_ref, g_ref, w_ref, wf_ref, bf_ref,
                   qa_ref, kp_ref, vt_ref, gm_ref, cv_ref, logf_ref, nrm_ref):
    tb = x_ref.shape[0]
    h = _rms(x_ref[...], g_ref[...])
    hb = h.astype(bf16)
    z = jnp.dot(hb, w_ref[...], preferred_element_type=f32)
    hw = ATT_HEADS * LANES
    lane = lax.broadcasted_iota(jnp.int32, (1, hw), 1) % LANES
    fcol = (lane >= HEAD_DIM) & (lane < HEAD_DIM + F_PARTS)
    zq = z[:, :hw] * (HEAD_DIM ** -0.5 * LOG2E)
    zk = z[:, hw:2 * hw]
    qa = jnp.where(fcol, -1.0, zq).astype(bf16)
    kp = zk.astype(bf16)
    vT = z[:, 2 * hw:2 * hw + ATT_WIDTH].T
    tail = jnp.where(lax.broadcasted_iota(jnp.int32, (V_ROWS - HEAD_DIM, tb), 0) == 0,
                     1.0, 0.0).astype(bf16)
    for hd in range(ATT_HEADS):
        qa_ref[0, hd] = qa[:, hd * LANES:(hd + 1) * LANES]
        kp_ref[0, hd] = kp[:, hd * LANES:(hd + 1) * LANES]
        vt_ref[0, hd, :HEAD_DIM, :] = vT[hd * HEAD_DIM:(hd + 1) * HEAD_DIM, :].astype(bf16)
        vt_ref[0, hd, HEAD_DIM:, :] = tail
    c0 = 2 * hw + ATT_WIDTH
    gm_ref[...] = z[:, c0:c0 + 2 * GMLP_WIDTH]
    cv_ref[...] = z[:, c0 + 2 * GMLP_WIDTH:]
    r = lax.broadcasted_iota(jnp.int32, (hw, LANES), 0) // LANES
    c = lax.broadcasted_iota(jnp.int32, (hw, LANES), 1)
    ind_q = jnp.where(r == c, 1.0, 0.0).astype(bf16)
    ind_k = jnp.where(r + ATT_HEADS == c, 1.0, 0.0).astype(bf16)
    nrm_ref[...] = (jnp.dot((zq * zq).astype(bf16), ind_q, preferred_element_type=f32)
                    + jnp.dot((zk * zk).astype(bf16), ind_k, preferred_element_type=f32))
    hl = (h - hb.astype(f32)).astype(bf16)
    wf = wf_ref[...]
    wfh = wf.astype(bf16)
    wfl = (wf - wfh.astype(f32)).astype(bf16)
    zf = (jnp.dot(hb, wfh, preferred_element_type=f32)
          + jnp.dot(hb, wfl, preferred_element_type=f32)
          + jnp.dot(hl, wfh, preferred_element_type=f32)) + bf_ref[...]
    logf_ref[...] = jnp.minimum(zf, 0.0) - jnp.log1p(jnp.exp(-jnp.abs(zf)))


def _in_proj(x2, g, w, wf, bfp, B, S, tb):
    T = x2.shape[0]
    bps = S // tb
    ncol = w.shape[1]
    const = lambda i: (0, 0)
    row = lambda i: (i, 0)
    head_spec = pl.BlockSpec((1, ATT_HEADS, tb, LANES), lambda i: (i // bps, 0, i % bps, 0))
    head_shape = jax.ShapeDtypeStruct((B, ATT_HEADS, S, LANES), bf16)
    return pl.pallas_call(
        _inproj_kernel,
        grid=(T // tb,),
        in_specs=[pl.BlockSpec((tb, D_MODEL), row),
                  pl.BlockSpec((1, D_MODEL), const),
                  pl.BlockSpec((D_MODEL, ncol), const),
                  pl.BlockSpec((D_MODEL, LANES), const),
                  pl.BlockSpec((1, LANES), const)],
        out_specs=[head_spec, head_spec,
                   pl.BlockSpec((1, ATT_HEADS, V_ROWS, tb), lambda i: (i // bps, 0, 0, i % bps)),
                   pl.BlockSpec((tb, 2 * GMLP_WIDTH), row),
                   pl.BlockSpec((tb, 3 * CONV_WIDTH), row),
                   pl.BlockSpec((tb, LANES), row),
                   pl.BlockSpec((tb, LANES), row)],
        out_shape=[head_shape, head_shape,
                   jax.ShapeDtypeStruct((B, ATT_HEADS, V_ROWS, S), bf16),
                   jax.ShapeDtypeStruct((T, 2 * GMLP_WIDTH), f32),
                   jax.ShapeDtypeStruct((T, 3 * CONV_WIDTH), f32),
                   jax.ShapeDtypeStruct((T, LANES), f32),
                   jax.ShapeDtypeStruct((T, LANES), f32)],
        compiler_params=_cparams(("parallel",)),
        name="in_proj",
    )(x2, g, w, wf, bfp)


def _in_proj_weights(wl):
    f0 = 3 * ATT_WIDTH

    def per_head(cols):
        w3 = cols.reshape(D_MODEL, ATT_HEADS, HEAD_DIM)
        return jnp.pad(w3, ((0, 0), (0, 0), (0, LANES - HEAD_DIM))).reshape(D_MODEL, -1)

    w_main = jnp.concatenate([per_head(wl[:, :ATT_WIDTH]), per_head(wl[:, ATT_WIDTH:2 * ATT_WIDTH]),
                              wl[:, 2 * ATT_WIDTH:f0], wl[:, f0 + ATT_HEADS:]], axis=1)
    wf = jnp.pad(wl[:, f0:f0 + ATT_HEADS], ((0, 0), (0, LANES - ATT_HEADS)))
    return w_main.astype(bf16), wf


def _cumsum_kernel(x_ref, kp_ref, ka_ref, carry_ref, *, tc):
    @pl.when(pl.program_id(1) == 0)
    def _():
        carry_ref[...] = jnp.zeros_like(carry_ref)

    row = lax.broadcasted_iota(jnp.int32, (tc, tc), 0)
    col = lax.broadcasted_iota(jnp.int32, (tc, tc), 1)
    tri = jnp.where(col <= row, 1.0, 0.0).astype(bf16)
    hi, mid, lo = _split3(x_ref[...])
    c = (jnp.dot(tri, hi, preferred_element_type=f32)
         + jnp.dot(tri, mid, preferred_element_type=f32)
         + jnp.dot(tri, lo, preferred_element_type=f32)) + carry_ref[...]
    carry_ref[...] = c[tc - 1:tc, :]
    hw = ATT_HEADS * LANES
    src = lax.broadcasted_iota(jnp.int32, (LANES, hw), 0)
    dst = lax.broadcasted_iota(jnp.int32, (LANES, hw), 1)
    fcols = jnp.zeros((tc, hw), f32)
    for p, part in enumerate(_split3(c * LOG2E)):
        place = jnp.where(dst == src * LANES + HEAD_DIM + p, 1.0, 0.0).astype(bf16)
        fcols = fcols + jnp.dot(part, place, preferred_element_type=f32)
    for hd in range(ATT_HEADS):
        ka_ref[0, hd] = (kp_ref[0, hd].astype(f32)
                         + fcols[:, hd * LANES:(hd + 1) * LANES]).astype(bf16)


def _cumsum(logf, kp, tc):
    B, H, S, _ = kp.shape
    nb = S // tc
    head_spec = pl.BlockSpec((1, H, tc, LANES), lambda b, j: (b, 0, j, 0))
    return pl.pallas_call(
        functools.partial(_cumsum_kernel, tc=tc),
        grid=(B, nb),
        in_specs=[pl.BlockSpec((tc, LANES), lambda b, j: (b * nb + j, 0)), head_spec],
        out_specs=head_spec,
        out_shape=jax.ShapeDtypeStruct(kp.shape, bf16),
        scratch_shapes=[pltpu.VMEM((1, LANES), f32)],
        compiler_params=_cparams(("arbitrary", "arbitrary")),
        name="cumsum_logf",
    )(logf, kp)


def _attn_kernel(jstart_ref, qa_ref, ka_ref, vt_ref, o_ref, acc_ref, m_ref, *, tq, hg):
    i = pl.program_id(2)
    m_ref[...] = jnp.full(m_ref.shape, NEG_BIG, f32)
    acc_ref[...] = jnp.zeros(acc_ref.shape, f32)

    def step(j, masked):
        start = pl.multiple_of(j * tq, tq)

        def scores(h):
            k = ka_ref[0, h, pl.ds(start, tq), :]
            s = lax.dot_general(k, qa_ref[0, h], _NT, preferred_element_type=f32)
            if masked:
                r = lax.broadcasted_iota(jnp.int32, (tq, tq), 0)
                c = lax.broadcasted_iota(jnp.int32, (tq, tq), 1)
                s = jnp.where(r <= c, s, NEG_BIG)
            return s

        def new_max(h, s):
            return jnp.maximum(m_ref[h], jnp.max(s, axis=0, keepdims=True))

        def weights(s, m_new):
            return jnp.exp2(s - m_new).astype(bf16)

        def accumulate(h, p, m_new):
            pv = jnp.dot(vt_ref[0, h, :, pl.ds(start, tq)], p, preferred_element_type=f32)
            acc_ref[h] = jnp.exp2(m_ref[h] - m_new) * acc_ref[h] + pv
            m_ref[h] = m_new

        s, mx, p = {}, {}, {}
        for t in range(hg + 3):
            if t < hg:
                s[t] = scores(t)
            if 0 <= t - 1 < hg:
                mx[t - 1] = new_max(t - 1, s[t - 1])
            if 0 <= t - 2 < hg:
                p[t - 2] = weights(s.pop(t - 2), mx[t - 2])
            if 0 <= t - 3 < hg:
                accumulate(t - 3, p.pop(t - 3), mx.pop(t - 3))

    def body(j, carry):
        step(j, False)
        return carry

    lax.fori_loop(jstart_ref[pl.program_id(0), pl.program_id(1), i], i, body, 0)
    step(i, True)
    for h in range(hg):
        o_ref[0, h * HEAD_DIM:(h + 1) * HEAD_DIM, :] = (
            acc_ref[h, :HEAD_DIM, :] / acc_ref[h, HEAD_DIM:HEAD_DIM + 1, :])


def _attention(jstart, qa, ka, vt, tq, hg):
    B, H, S, _ = qa.shape
    return pl.pallas_call(
        functools.partial(_attn_kernel, tq=tq, hg=hg),
        grid_spec=pltpu.PrefetchScalarGridSpec(
            num_scalar_prefetch=1,
            grid=(B, H // hg, S // tq),
            in_specs=[pl.BlockSpec((1, hg, tq, LANES), lambda b, g, i, js: (b, g, i, 0)),
                      pl.BlockSpec((1, hg, S, LANES), lambda b, g, i, js: (b, g, 0, 0),
                                   pipeline_mode=pl.Buffered(1)),
                      pl.BlockSpec((1, hg, V_ROWS, S), lambda b, g, i, js: (b, g, 0, 0),
                                   pipeline_mode=pl.Buffered(1))],
            out_specs=pl.BlockSpec((1, hg * HEAD_DIM, tq), lambda b, g, i, js: (b, g, i)),
            scratch_shapes=[pltpu.VMEM((hg, V_ROWS, tq), f32), pltpu.VMEM((hg, 1, tq), f32)]),
        out_shape=jax.ShapeDtypeStruct((B, H * HEAD_DIM, S), f32),
        compiler_params=_cparams(("parallel", "parallel", "arbitrary")),
        name="fox_attention",
    )(jstart, qa, ka, vt)


def _first_key_block(nrm, ka, tq, hg):
    B, H, S, _ = ka.shape
    nq = S // tq
    n2 = jnp.max(nrm.reshape(B, nq, tq, LANES), axis=2)
    qmax = jnp.sqrt(n2[:, :, :H]).transpose(0, 2, 1) * NORM_SLACK
    kmax = jnp.sqrt(n2[:, :, H:2 * H]).transpose(0, 2, 1) * NORM_SLACK
    blk = jnp.arange(nq, dtype=jnp.int32)
    earlier = blk[None, :] <= blk[:, None]
    kcum = jnp.max(jnp.where(earlier[None, None], kmax[:, :, None, :], 0.0), axis=3)
    fl = ka[:, :, :, HEAD_DIM:HEAD_DIM + F_PARTS].astype(f32).sum(-1)
    f_end = fl[:, :, tq - 1::tq]
    f_start = fl[:, :, ::tq]
    upper = qmax[:, :, :, None] * kcum[:, :, None, :] - f_end[:, :, None, :]
    lower = -qmax * kmax - f_start
    skip = upper < lower[:, :, :, None] - UNDERFLOW_LOG2
    lead = jnp.min(jnp.where(skip, nq, blk[None, None, None, :]), axis=3)
    lead = jnp.minimum(lead, blk[None, None, :])
    return lead.reshape(B, H // hg, hg, nq).min(axis=2).astype(jnp.int32)


def _mixout_kernel(oatt_ref, gm_ref, cv_ref, cvp_ref, x_ref, sgug_ref, wsp_ref, bsp_ref,
                   cw_ref, mg_ref, wout_ref, n2g_ref, xo_ref, h2_ref, *, tb, blocks_per_seq):
    i = pl.program_id(0)
    gm = gm_ref[...]
    u = _gelu(gm[:, :GMLP_WIDTH])
    v = _gelu(gm[:, GMLP_WIDTH:])
    gr = lax.broadcasted_iota(jnp.int32, (GMLP_WIDTH, GMLP_WIDTH), 0) // GROUP_DIM
    gc = lax.broadcasted_iota(jnp.int32, (GMLP_WIDTH, GMLP_WIDTH), 1) // GROUP_DIM
    grp = jnp.where(gr == gc, 1.0 / GROUP_DIM, 0.0).astype(bf16)
    vh, vm, vl = _split3(v * v)
    ms = (jnp.dot(vh, grp, preferred_element_type=f32)
          + jnp.dot(vm, grp, preferred_element_type=f32)
          + jnp.dot(vl, grp, preferred_element_type=f32))
    vn = (v * lax.rsqrt(ms + EPS) * sgug_ref[...]).astype(bf16)
    tr = lax.broadcasted_iota(jnp.int32, (CHUNK, CHUNK), 0)
    tc = lax.broadcasted_iota(jnp.int32, (CHUNK, CHUNK), 1)
    lane_g = lax.broadcasted_iota(jnp.int32, (CHUNK, GMLP_WIDTH), 1) // GROUP_DIM
    mixed_chunks = []
    for c in range(tb // CHUNK):
        vc = vn[c * CHUNK:(c + 1) * CHUNK, :]
        mixed = jnp.zeros((CHUNK, GMLP_WIDTH), f32)
        for g in range(GMLP_GROUPS):
            wg = jnp.where(tc <= tr, wsp_ref[g], 0.0).astype(bf16)
            mg = jnp.dot(wg, vc, preferred_element_type=f32)
            mixed = jnp.where(lane_g == g, mg, mixed)
        mixed_chunks.append(mixed + bsp_ref[...])
    o_gmlp = u * jnp.concatenate(mixed_chunks, axis=0)
    cv = cv_ref[...]
    c_in = cv[:, :CONV_WIDTH]
    c_b = cv[:, CONV_WIDTH:2 * CONV_WIDTH]
    c_c = cv[:, 2 * CONV_WIDTH:]
    z = c_c * c_in
    cvp = cvp_ref[...]
    zp = cvp[:, 2 * CONV_WIDTH:] * cvp[:, :CONV_WIDTH]
    zp = jnp.where(i % blocks_per_seq == 0, jnp.zeros_like(zp), zp)
    rowi = lax.broadcasted_iota(jnp.int32, (tb, CONV_WIDTH), 0)
    z1 = pltpu.roll(z, 1, axis=0)
    z1 = jnp.where(rowi == 0, zp[7:8, :], z1)
    z2 = pltpu.roll(z, 2, axis=0)
    z2 = jnp.where(rowi == 0, zp[6:7, :], z2)
    z2 = jnp.where(rowi == 1, zp[7:8, :], z2)
    cw = cw_ref[...]
    o_conv = c_b * (cw[0:1, :] * z2 + cw[1:2, :] * z1 + cw[2:3, :] * z)
    mg_ = mg_ref[...]
    o_att = oatt_ref[0].T
    mixed_all = jnp.concatenate([
        _rms(o_att, mg_[:, :ATT_WIDTH]),
        _rms(o_gmlp, mg_[:, ATT_WIDTH:ATT_WIDTH + GMLP_WIDTH]),
        _rms(o_conv, mg_[:, ATT_WIDTH + GMLP_WIDTH:])], axis=-1).astype(bf16)
    xn = x_ref[...] + jnp.dot(mixed_all, wout_ref[...], preferred_element_type=f32)
    xo_ref[...] = xn
    h2_ref[...] = _rms(xn, n2g_ref[...]).T.astype(bf16)


def _mix_out(oatt_t, gm, cv, x2, sgug, wsp, bspf, cw, mg, wout, n2g, S, tb):
    T = x2.shape[0]
    bps = S // tb
    const2 = lambda i: (0, 0)
    row = lambda i: (i, 0)
    return pl.pallas_call(
        functools.partial(_mixout_kernel, tb=tb, blocks_per_seq=bps),
        grid=(T // tb,),
        in_specs=[pl.BlockSpec((1, ATT_WIDTH, tb), lambda i: (i // bps, 0, i % bps)),
                  pl.BlockSpec((tb, 2 * GMLP_WIDTH), row),
                  pl.BlockSpec((tb, 3 * CONV_WIDTH), row),
                  pl.BlockSpec((8, 3 * CONV_WIDTH),
                               lambda i: (jnp.maximum(i * (tb // 8) - 1, 0), 0)),
                  pl.BlockSpec((tb, D_MODEL), row),
                  pl.BlockSpec((1, GMLP_WIDTH), const2),
                  pl.BlockSpec((GMLP_GROUPS, CHUNK, CHUNK), lambda i: (0, 0, 0)),
                  pl.BlockSpec((CHUNK, GMLP_WIDTH), const2),
                  pl.BlockSpec((8, CONV_WIDTH), const2),
                  pl.BlockSpec((1, MIX_WIDTH), const2),
                  pl.BlockSpec((MIX_WIDTH, D_MODEL), const2),
                  pl.BlockSpec((1, D_MODEL), const2)],
        out_specs=[pl.BlockSpec((tb, D_MODEL), row),
                   pl.BlockSpec((D_MODEL, tb), lambda i: (0, i))],
        out_shape=[jax.ShapeDtypeStruct((T, D_MODEL), f32),
                   jax.ShapeDtypeStruct((D_MODEL, T), bf16)],
        compiler_params=_cparams(("parallel",)),
        name="mix_out",
    )(oatt_t, gm, cv, cv, x2, sgug, wsp, bspf, cw, mg, wout, n2g)


def _colmax(x):
    return jnp.max(x, axis=0, keepdims=True)


def _colsum(x):
    return jnp.sum(x, axis=0, keepdims=True)


def _tree(op, xs):
    while len(xs) > 1:
        xs = [op(xs[i], xs[i + 1]) if i + 1 < len(xs) else xs[i] for i in range(0, len(xs), 2)]
    return xs[0]


def _top16(scores, vals_refs):
    scores = list(scores)
    for a in range(PEER_TOPK):
        for i, s in enumerate(scores):
            m = _colmax(s)
            scores[i] = jnp.where(s == m, RANK_MARK * (1.0 + a * RANK_STEP), s)
            vals_refs[i][a:a + 1, :] = m
    ranks = []
    for s in scores:
        rank = jnp.round((s - RANK_MARK) * (1.0 / (RANK_MARK * RANK_STEP)))
        ranks.append(jnp.where(s <= RANK_MARK, rank, float(PEER_TOPK)))
    return ranks


def _select_tables(s1, s2, v1_ref, v2_ref):
    r1, r2 = _top16([s1, s2], [v1_ref, v2_ref])
    v1 = v1_ref[...]
    v2 = v2_ref[...]
    cands = [v1[0:1] + v2[0:8]]
    cands += [v1[a:a + 1] + v2[0:8] for a in range(1, 8)]
    cands += [v1[0:1] + v2[8:16], v1[8:16] + v2[0:1]]
    cmax = v1[0:1] + v2[0:1]
    work = list(cands)
    tau = cmax
    cnt = jnp.zeros_like(cmax)
    for _ in range(PEER_TOPK):
        m = _colmax(_tree(jnp.maximum, work))
        tau = jnp.where(cnt < float(PEER_TOPK), m, tau)
        hits = [w == m for w in work]
        cnt = cnt + _colsum(_tree(jnp.add, [jnp.where(h, 1.0, 0.0) for h in hits]))
        work = [jnp.where(h, -jnp.inf, w) for h, w in zip(hits, work)]
    sel = [c >= tau for c in cands]
    zsum = _colsum(_tree(jnp.add, [jnp.where(s_, jnp.exp(c - cmax), 0.0)
                                   for s_, c in zip(sel, cands)]))
    ones = [jnp.where(s_, 1.0, 0.0) for s_ in sel]
    n_lo = [_colsum(ones[0] + ones[8])] + [_colsum(o) for o in ones[1:8]]
    n_hi = ones[9]
    n1 = jnp.zeros(r1.shape, f32)
    for a in range(8):
        n1 = jnp.where(r1 == float(a), n_lo[a], n1)
    for a in range(8, PEER_TOPK):
        n1 = jnp.where(r1 == float(a), n_hi[a - 8:a - 7], n1)
    e2 = jnp.exp(s2 - v2[0:1])
    c1 = jnp.exp(s1 - v1[0:1]) / zsum
    return r2.astype(bf16), e2.astype(bf16), n1, c1


def _peerprep_kernel(h2_ref, wq_ref, keys_ref, r2_ref, e2_ref, n1_ref, c1_ref,
                     v1_ref, v2_ref):
    qT = jnp.dot(wq_ref[...], h2_ref[...], preferred_element_type=f32)
    qb = qT.astype(bf16)
    s1 = jnp.dot(keys_ref[0], qb[:D_HALF], preferred_element_type=f32)
    s2 = jnp.dot(keys_ref[1], qb[D_HALF:], preferred_element_type=f32)
    lpc = SLAB // LANES
    for c in range(r2_ref.shape[1]):
        r2s, e2s = [], []
        for k in range(lpc):
            t = c * lpc + k
            ls = slice(t * LANES, (t + 1) * LANES)
            r2, e2, n1, c1 = _select_tables(s1[:, ls], s2[:, ls], v1_ref, v2_ref)
            r2s.append(r2)
            e2s.append(e2)
            n1_ref[0, t] = n1
            c1_ref[0, t] = c1
        r2_ref[0, c] = jnp.concatenate(r2s, axis=1)
        e2_ref[0, c] = jnp.concatenate(e2s, axis=1)


def _peer_prep(h2t, wqT, keys, tb):
    T = h2t.shape[1]
    spec = pl.BlockSpec((1, tb // SLAB, N_KEYS, SLAB), lambda t, h: (h, t, 0, 0))
    shp = (PEER_HEADS, T // SLAB, N_KEYS, SLAB)
    rspec = pl.BlockSpec((1, tb // LANES, N_KEYS, LANES), lambda t, h: (h, t, 0, 0))
    rshp = (PEER_HEADS, T // LANES, N_KEYS, LANES)
    return pl.pallas_call(
        _peerprep_kernel,
        grid=(T // tb, PEER_HEADS),
        in_specs=[pl.BlockSpec((D_MODEL, tb), lambda t, h: (0, t)),
                  pl.BlockSpec((D_KEY, D_MODEL), lambda t, h: (h, 0)),
                  pl.BlockSpec((2, N_KEYS, D_HALF), lambda t, h: (0, 0, 0))],
        out_specs=[spec, spec, rspec, rspec],
        out_shape=[jax.ShapeDtypeStruct(shp, bf16), jax.ShapeDtypeStruct(shp, bf16),
                   jax.ShapeDtypeStruct(rshp, f32), jax.ShapeDtypeStruct(rshp, f32)],
        scratch_shapes=[pltpu.VMEM((PEER_TOPK, LANES), f32), pltpu.VMEM((PEER_TOPK, LANES), f32)],
        compiler_params=_cparams(("parallel", "arbitrary")),
        name="peer_prep",
    )(h2t, wqT, keys)


def _peermain_kernel(h2_ref, u_ref, vt_ref, r2_ref, e2_ref, n1_ref, c1_ref, x_ref, fg_ref,
                     o_ref, acc_ref, *, rows, final):
    e = pl.program_id(1)
    tb = h2_ref.shape[1]
    nc = tb // SLAB
    lpc = SLAB // LANES

    @pl.when(e == 0)
    def _():
        acc_ref[...] = jnp.zeros_like(acc_ref)

    base = e * rows

    def row_tile(ref, h, r, c):
        row = jnp.concatenate(
            [jnp.broadcast_to(ref[h, c * lpc + k, pl.ds(base + r, 1), :], (BF16_ROWS, LANES))
             for k in range(lpc)], axis=1).astype(bf16)
        return jnp.tile(row, (N_KEYS // BF16_ROWS, 1))

    zero = jnp.zeros((N_KEYS, SLAB), bf16)

    ub = UNIT_ROWS * N_KEYS
    units = [(c, u) for c in range(nc) for u in range(rows // UNIT_ROWS)]

    def activations(unit):
        c, u = unit
        return jnp.dot(u_ref[u * ub:(u + 1) * ub, :], h2_ref[:, c * SLAB:(c + 1) * SLAB],
                       preferred_element_type=f32)

    def gates(unit):
        c, u = unit
        out = []
        for r in range(u * UNIT_ROWS, (u + 1) * UNIT_ROWS):
            g = None
            for h in range(PEER_HEADS):
                term = jnp.where(r2_ref[h, c] < row_tile(n1_ref, h, r, c), e2_ref[h, c], zero)
                term = term * row_tile(c1_ref, h, r, c)
                g = term if g is None else g + term
            out.append(g)
        return jnp.concatenate(out, axis=0)

    upc = rows // UNIT_ROWS

    def accumulate(c, w):
        cs = slice(c * SLAB, (c + 1) * SLAB)
        acc_ref[:, cs] += jnp.dot(vt_ref[...], w, preferred_element_type=f32)

    a, g, w = {}, {}, {}
    for t in range(len(units) + 3):
        if t < len(units):
            a[t] = activations(units[t])
        if 0 <= t - 1 < len(units):
            g[t - 1] = gates(units[t - 1])
        if 0 <= t - 2 < len(units):
            w[t - 2] = _gelu(a.pop(t - 2).astype(bf16)) * g.pop(t - 2)
        if 0 <= t - 3 < len(units) and (t - 3) % upc == upc - 1:
            c = (t - 3) // upc
            accumulate(c, jnp.concatenate([w.pop(c * upc + k) for k in range(upc)], axis=0))

    @pl.when(e == pl.num_programs(1) - 1)
    def _():
        y = x_ref[...] + acc_ref[...].T
        o_ref[...] = _rms(y, fg_ref[...]) if final else y


def _peer_main(h2t, u, vt, r2, e2, n1, c1, x2, fg, layer, tb, rows, final):
    T = h2t.shape[1]
    eb = rows * N_KEYS
    sel = pl.BlockSpec((PEER_HEADS, tb // SLAB, N_KEYS, SLAB), lambda t, e: (0, t, 0, 0))
    rsel = pl.BlockSpec((PEER_HEADS, tb // LANES, N_KEYS, LANES), lambda t, e: (0, t, 0, 0))
    return pl.pallas_call(
        functools.partial(_peermain_kernel, rows=rows, final=final),
        grid=(T // tb, N_KEYS // rows),
        in_specs=[pl.BlockSpec((D_MODEL, tb), lambda t, e: (0, t)),
                  pl.BlockSpec((None, eb, D_MODEL), lambda t, e: (layer, e, 0)),
                  pl.BlockSpec((None, D_MODEL, eb), lambda t, e: (layer, 0, e)),
                  sel, sel, rsel, rsel,
                  pl.BlockSpec((tb, D_MODEL), lambda t, e: (t, 0)),
                  pl.BlockSpec((1, D_MODEL), lambda t, e: (0, 0))],
        out_specs=pl.BlockSpec((tb, D_MODEL), lambda t, e: (t, 0)),
        out_shape=jax.ShapeDtypeStruct((T, D_MODEL), f32),
        scratch_shapes=[pltpu.VMEM((D_MODEL, tb), f32)],
        compiler_params=_cparams(("parallel", "arbitrary")),
        name="peer_main",
    )(h2t, u, vt, r2, e2, n1, c1, x2, fg)


def _block_sizes(S):
    return dict(tb_in=min(512, S), tc=min(512, S), tq=min(512, S), heads_per_step=8,
                tb_mix=min(256, S), tb_prep=min(512, S), tb_peer=min(512, S), rows=16)


def kernel(x, norm1_g, w_in, b_forget, sgu_norm_g, w_spatial, b_spatial, conv_w,
           mix_norm_g, w_out, norm2_g, w_query, sub_keys, expert_u, expert_v, final_g):
    B, S, D = x.shape
    T = B * S
    depth = w_in.shape[0]
    bs = _block_sizes(S)
    x2 = x.reshape(T, D)
    u_all = expert_u.astype(bf16)
    vt_all = expert_v.transpose(0, 2, 1).astype(bf16)
    for l in range(depth):
        w_main, wf = _in_proj_weights(w_in[l])
        bfp = jnp.pad(b_forget[l], (0, LANES - ATT_HEADS)).reshape(1, LANES)
        qa, kp, vt, gm, cv, logf, nrm = _in_proj(x2, norm1_g[l].reshape(1, D), w_main, wf, bfp,
                                                 B, S, bs["tb_in"])
        ka = _cumsum(logf, kp, bs["tc"])
        jstart = _first_key_block(nrm, ka, bs["tq"], bs["heads_per_step"])
        oatt_t = _attention(jstart, qa, ka, vt, bs["tq"], bs["heads_per_step"])
        bspf = jnp.repeat(b_spatial[l].T, GROUP_DIM, axis=1)
        cw = jnp.pad(conv_w[l], ((0, 8 - CONV_K), (0, 0)))
        x2, h2t = _mix_out(oatt_t, gm, cv, x2, sgu_norm_g[l].reshape(1, GMLP_WIDTH),
                          w_spatial[l], bspf, cw, mix_norm_g[l].reshape(1, MIX_WIDTH),
                          w_out[l].astype(bf16), norm2_g[l].reshape(1, D), S, bs["tb_mix"])
        r2, e2, n1, c1 = _peer_prep(h2t, w_query[l].T.astype(bf16), sub_keys[l].astype(bf16),
                                    bs["tb_prep"])
        x2 = _peer_main(h2t, u_all, vt_all, r2, e2, n1, c1, x2, final_g.reshape(1, D), l,
                        bs["tb_peer"], bs["rows"], final=(l == depth - 1))
    return x2.reshape(B, S, D)
```

```python
import functools
import math

import jax
import jax.numpy as jnp
from jax import lax
from jax.experimental import pallas as pl
from jax.experimental.pallas import tpu as pltpu

f32 = jnp.float32
bf16 = jnp.bfloat16

D_MODEL = 1024
ATT_WIDTH = 512
ATT_HEADS = 8
HEAD_DIM = 64
GMLP_WIDTH = 256
GMLP_GROUPS = 4
GROUP_DIM = 64
CHUNK = 128
CONV_WIDTH = 256
CONV_K = 3
MIX_WIDTH = 1024
PEER_HEADS = 8
N_KEYS = 128
PEER_TOPK = 16
D_KEY = 256
D_HALF = 128
EPS = 1e-6
LANES = 128
BF16_ROWS = 16
SLAB = 256
UNIT_ROWS = 8
NEG_BIG = -1e30
LOG2E = math.log2(math.e)
GELU_K0 = -2.0 * math.sqrt(2.0 / math.pi)
GELU_K1 = GELU_K0 * 0.044715
F_PARTS = 3
V_ROWS = 80
UNDERFLOW_LOG2 = 160.0
NORM_SLACK = 1.02
RANK_MARK = -3.0e38
RANK_STEP = 2.0 ** -10

VMEM_LIMIT = 56 * 1024 * 1024

_NT = (((1,), (1,)), ((), ()))


def _cparams(sem, flags=None):
    return pltpu.CompilerParams(dimension_semantics=sem, vmem_limit_bytes=VMEM_LIMIT, flags=flags)


def _gelu(x):
    return x / (1.0 + jnp.exp(x * (GELU_K0 + GELU_K1 * (x * x))))


def _rms(x, g):
    return x * lax.rsqrt(jnp.mean(x * x, axis=-1, keepdims=True) + EPS) * g


def _split3(x):
    hi = x.astype(bf16)
    r = x - hi.astype(f32)
    mid = r.astype(bf16)
    lo = (r - mid.astype(f32)).astype(bf16)
    return hi, mid, lo


def _inproj_kernel(x_ref, g_ref, w_ref, wf_ref, bf_ref,
                   qa_ref, kp_ref, vt_ref, gm_ref, cv_ref, logf_ref, nrm_ref):
    tb = x_ref.shape[0]
    h = _rms(x_ref[...], g_ref[...])
    hb = h.astype(bf16)
    z = jnp.dot(hb, w_ref[...], preferred_element_type=f32)
    hw = ATT_HEADS * LANES
    lane = lax.broadcasted_iota(jnp.int32, (1, hw), 1) % LANES
    fcol = (lane >= HEAD_DIM) & (lane < HEAD_DIM + F_PARTS)
    zq = z[:, :hw] * (HEAD_DIM ** -0.5 * LOG2E)
    zk = z[:, hw:2 * hw]
    qa = jnp.where(fcol, -1.0, zq).astype(bf16)
    kp = zk.astype(bf16)
    vT = z[:, 2 * hw:2 * hw + ATT_WIDTH].T
    tail = jnp.where(lax.broadcasted_iota(jnp.int32, (V_ROWS - HEAD_DIM, tb), 0) == 0,
                     1.0, 0.0).astype(bf16)
    for hd in range(ATT_HEADS):
        qa_ref[0, hd] = qa[:, hd * LANES:(hd + 1) * LANES]
        kp_ref[0, hd] = kp[:, hd * LANES:(hd + 1) * LANES]
        vt_ref[0, hd, :HEAD_DIM, :] = vT[hd * HEAD_DIM:(hd + 1) * HEAD_DIM, :].astype(bf16)
        vt_ref[0, hd, HEAD_DIM:, :] = tail
    c0 = 2 * hw + ATT_WIDTH
    gm_ref[...] = z[:, c0:c0 + 2 * GMLP_WIDTH]
    cv_ref[...] = z[:, c0 + 2 * GMLP_WIDTH:]
    r = lax.broadcasted_iota(jnp.int32, (hw, LANES), 0) // LANES
    c = lax.broadcasted_iota(jnp.int32, (hw, LANES), 1)
    ind_q = jnp.where(r == c, 1.0, 0.0).astype(bf16)
    ind_k = jnp.where(r + ATT_HEADS == c, 1.0, 0.0).astype(bf16)
    nrm_ref[...] = (jnp.dot((zq * zq).astype(bf16), ind_q, preferred_element_type=f32)
                    + jnp.dot((zk * zk).astype(bf16), ind_k, preferred_element_type=f32))
    hl = (h - hb.astype(f32)).astype(bf16)
    wf = wf_ref[...]
    wfh = wf.astype(bf16)
    wfl = (wf - wfh.astype(f32)).astype(bf16)
    zf = (jnp.dot(hb, wfh, preferred_element_type=f32)
          + jnp.dot(hb, wfl, preferred_element_type=f32)
          + jnp.dot(hl, wfh, preferred_element_type=f32)) + bf_ref[...]
    logf_ref[...] = jnp.minimum(zf, 0.0) - jnp.log1p(jnp.exp(-jnp.abs(zf)))


def _in_proj(x2, g, w, wf, bfp, B, S, tb):
    T = x2.shape[0]
    bps = S // tb
    ncol = w.shape[1]
    const = lambda i: (0, 0)
    row = lambda i: (i, 0)
    head_spec = pl.BlockSpec((1, ATT_HEADS, tb, LANES), lambda i: (i // bps, 0, i % bps, 0))
    head_shape = jax.ShapeDtypeStruct((B, ATT_HEADS, S, LANES), bf16)
    return pl.pallas_call(
        _inproj_kernel,
        grid=(T // tb,),
        in_specs=[pl.BlockSpec((tb, D_MODEL), row),
                  pl.BlockSpec((1, D_MODEL), const),
                  pl.BlockSpec((D_MODEL, ncol), const),
                  pl.BlockSpec((D_MODEL, LANES), const),
                  pl.BlockSpec((1, LANES), const)],
        out_specs=[head_spec, head_spec,
                   pl.BlockSpec((1, ATT_HEADS, V_ROWS, tb), lambda i: (i // bps, 0, 0, i % bps)),
                   pl.BlockSpec((tb, 2 * GMLP_WIDTH), row),
                   pl.BlockSpec((tb, 3 * CONV_WIDTH), row),
                   pl.BlockSpec((tb, LANES), row),
                   pl.BlockSpec((tb, LANES), row)],
        out_shape=[head_shape, head_shape,
                   jax.ShapeDtypeStruct((B, ATT_HEADS, V_ROWS, S), bf16),
                   jax.ShapeDtypeStruct((T, 2 * GMLP_WIDTH), f32),
                   jax.ShapeDtypeStruct((T, 3 * CONV_WIDTH), f32),
                   jax.ShapeDtypeStruct((T, LANES), f32),
                   jax.ShapeDtypeStruct((T, LANES), f32)],
        compiler_params=_cparams(("parallel",)),
        name="in_proj",
    )(x2, g, w, wf, bfp)


def _in_proj_weights(wl):
    f0 = 3 * ATT_WIDTH

    def per_head(cols):
        w3 = cols.reshape(D_MODEL, ATT_HEADS, HEAD_DIM)
        return jnp.pad(w3, ((0, 0), (0, 0), (0, LANES - HEAD_DIM))).reshape(D_MODEL, -1)

    w_main = jnp.concatenate([per_head(wl[:, :ATT_WIDTH]), per_head(wl[:, ATT_WIDTH:2 * ATT_WIDTH]),
                              wl[:, 2 * ATT_WIDTH:f0], wl[:, f0 + ATT_HEADS:]], axis=1)
    wf = jnp.pad(wl[:, f0:f0 + ATT_HEADS], ((0, 0), (0, LANES - ATT_HEADS)))
    return w_main.astype(bf16), wf


def _cumsum_kernel(x_ref, kp_ref, ka_ref, carry_ref, *, tc):
    @pl.when(pl.program_id(1) == 0)
    def _():
        carry_ref[...] = jnp.zeros_like(carry_ref)

    row = lax.broadcasted_iota(jnp.int32, (tc, tc), 0)
    col = lax.broadcasted_iota(jnp.int32, (tc, tc), 1)
    tri = jnp.where(col <= row, 1.0, 0.0).astype(bf16)
    hi, mid, lo = _split3(x_ref[...])
    c = (jnp.dot(tri, hi, preferred_element_type=f32)
         + jnp.dot(tri, mid, preferred_element_type=f32)
         + jnp.dot(tri, lo, preferred_element_type=f32)) + carry_ref[...]
    carry_ref[...] = c[tc - 1:tc, :]
    hw = ATT_HEADS * LANES
    src = lax.broadcasted_iota(jnp.int32, (LANES, hw), 0)
    dst = lax.broadcasted_iota(jnp.int32, (LANES, hw), 1)
    fcols = jnp.zeros((tc, hw), f32)
    for p, part in enumerate(_split3(c * LOG2E)):
        place = jnp.where(dst == src * LANES + HEAD_DIM + p, 1.0, 0.0).astype(bf16)
        fcols = fcols + jnp.dot(part, place, preferred_element_type=f32)
    for hd in range(ATT_HEADS):
        ka_ref[0, hd] = (kp_ref[0, hd].astype(f32)
                         + fcols[:, hd * LANES:(hd + 1) * LANES]).astype(bf16)


def _cumsum(logf, kp, tc):
    B, H, S, _ = kp.shape
    nb = S // tc
    head_spec = pl.BlockSpec((1, H, tc, LANES), lambda b, j: (b, 0, j, 0))
    return pl.pallas_call(
        functools.partial(_cumsum_kernel, tc=tc),
        grid=(B, nb),
        in_specs=[pl.BlockSpec((tc, LANES), lambda b, j: (b * nb + j, 0)), head_spec],
        out_specs=head_spec,
        out_shape=jax.ShapeDtypeStruct(kp.shape, bf16),
        scratch_shapes=[pltpu.VMEM((1, LANES), f32)],
        compiler_params=_cparams(("arbitrary", "arbitrary")),
        name="cumsum_logf",
    )(logf, kp)


def _attn_kernel(jstart_ref, qa_ref, ka_ref, vt_ref, o_ref, acc_ref, m_ref, *, tq, hg):
    i = pl.program_id(2)
    m_ref[...] = jnp.full(m_ref.shape, NEG_BIG, f32)
    acc_ref[...] = jnp.zeros(acc_ref.shape, f32)

    def step(j, masked):
        start = pl.multiple_of(j * tq, tq)

        def scores(h):
            k = ka_ref[0, h, pl.ds(start, tq), :]
            s = lax.dot_general(k, qa_ref[0, h], _NT, preferred_element_type=f32)
            if masked:
                r = lax.broadcasted_iota(jnp.int32, (tq, tq), 0)
                c = lax.broadcasted_iota(jnp.int32, (tq, tq), 1)
                s = jnp.where(r <= c, s, NEG_BIG)
            return s

        def new_max(h, s):
            return jnp.maximum(m_ref[h], jnp.max(s, axis=0, keepdims=True))

        def weights(s, m_new):
            return jnp.exp2(s - m_new).astype(bf16)

        def accumulate(h, p, m_new):
            pv = jnp.dot(vt_ref[0, h, :, pl.ds(start, tq)], p, preferred_element_type=f32)
            acc_ref[h] = jnp.exp2(m_ref[h] - m_new) * acc_ref[h] + pv
            m_ref[h] = m_new

        s, mx, p = {}, {}, {}
        for t in range(hg + 3):
            if t < hg:
                s[t] = scores(t)
            if 0 <= t - 1 < hg:
                mx[t - 1] = new_max(t - 1, s[t - 1])
            if 0 <= t - 2 < hg:
                p[t - 2] = weights(s.pop(t - 2), mx[t - 2])
            if 0 <= t - 3 < hg:
                accumulate(t - 3, p.pop(t - 3), mx.pop(t - 3))

    def body(j, carry):
        step(j, False)
        return carry

    lax.fori_loop(jstart_ref[pl.program_id(0), pl.program_id(1), i], i, body, 0)
    step(i, True)
    for h in range(hg):
        o_ref[0, h * HEAD_DIM:(h + 1) * HEAD_DIM, :] = (
            acc_ref[h, :HEAD_DIM, :] / acc_ref[h, HEAD_DIM:HEAD_DIM + 1, :])


def _attention(jstart, qa, ka, vt, tq, hg):
    B, H, S, _ = qa.shape
    return pl.pallas_call(
        functools.partial(_attn_kernel, tq=tq, hg=hg),
        grid_spec=pltpu.PrefetchScalarGridSpec(
            num_scalar_prefetch=1,
            grid=(B, H // hg, S // tq),
            in_specs=[pl.BlockSpec((1, hg, tq, LANES), lambda b, g, i, js: (b, g, i, 0)),
                      pl.BlockSpec((1, hg, S, LANES), lambda b, g, i, js: (b, g, 0, 0),
                                   pipeline_mode=pl.Buffered(1)),
                      pl.BlockSpec((1, hg, V_ROWS, S), lambda b, g, i, js: (b, g, 0, 0),
                                   pipeline_mode=pl.Buffered(1))],
            out_specs=pl.BlockSpec((1, hg * HEAD_DIM, tq), lambda b, g, i, js: (b, g, i)),
            scratch_shapes=[pltpu.VMEM((hg, V_ROWS, tq), f32), pltpu.VMEM((hg, 1, tq), f32)]),
        out_shape=jax.ShapeDtypeStruct((B, H * HEAD_DIM, S), f32),
        compiler_params=_cparams(("parallel", "parallel", "arbitrary")),
        name="fox_attention",
    )(jstart, qa, ka, vt)


def _first_key_block(nrm, ka, tq, hg):
    B, H, S, _ = ka.shape
    nq = S // tq
    n2 = jnp.max(nrm.reshape(B, nq, tq, LANES), axis=2)
    qmax = jnp.sqrt(n2[:, :, :H]).transpose(0, 2, 1) * NORM_SLACK
    kmax = jnp.sqrt(n2[:, :, H:2 * H]).transpose(0, 2, 1) * NORM_SLACK
    blk = jnp.arange(nq, dtype=jnp.int32)
    earlier = blk[None, :] <= blk[:, None]
    kcum = jnp.max(jnp.where(earlier[None, None], kmax[:, :, None, :], 0.0), axis=3)
    fl = ka[:, :, :, HEAD_DIM:HEAD_DIM + F_PARTS].astype(f32).sum(-1)
    f_end = fl[:, :, tq - 1::tq]
    f_start = fl[:, :, ::tq]
    upper = qmax[:, :, :, None] * kcum[:, :, None, :] - f_end[:, :, None, :]
    lower = -qmax * kmax - f_start
    skip = upper < lower[:, :, :, None] - UNDERFLOW_LOG2
    lead = jnp.min(jnp.where(skip, nq, blk[None, None, None, :]), axis=3)
    lead = jnp.minimum(lead, blk[None, None, :])
    return lead.reshape(B, H // hg, hg, nq).min(axis=2).astype(jnp.int32)


def _mixout_kernel(oatt_ref, gm_ref, cv_ref, cvp_ref, x_ref, sgug_ref, wsp_ref, bsp_ref,
                   cw_ref, mg_ref, wout_ref, n2g_ref, xo_ref, h2_ref, *, tb, blocks_per_seq):
    i = pl.program_id(0)
    gm = gm_ref[...]
    u = _gelu(gm[:, :GMLP_WIDTH])
    v = _gelu(gm[:, GMLP_WIDTH:])
    gr = lax.broadcasted_iota(jnp.int32, (GMLP_WIDTH, GMLP_WIDTH), 0) // GROUP_DIM
    gc = lax.broadcasted_iota(jnp.int32, (GMLP_WIDTH, GMLP_WIDTH), 1) // GROUP_DIM
    grp = jnp.where(gr == gc, 1.0 / GROUP_DIM, 0.0).astype(bf16)
    vh, vm, vl = _split3(v * v)
    ms = (jnp.dot(vh, grp, preferred_element_type=f32)
          + jnp.dot(vm, grp, preferred_element_type=f32)
          + jnp.dot(vl, grp, preferred_element_type=f32))
    vn = (v * lax.rsqrt(ms + EPS) * sgug_ref[...]).astype(bf16)
    tr = lax.broadcasted_iota(jnp.int32, (CHUNK, CHUNK), 0)
    tc = lax.broadcasted_iota(jnp.int32, (CHUNK, CHUNK), 1)
    lane_g = lax.broadcasted_iota(jnp.int32, (CHUNK, GMLP_WIDTH), 1) // GROUP_DIM
    mixed_chunks = []
    for c in range(tb // CHUNK):
        vc = vn[c * CHUNK:(c + 1) * CHUNK, :]
        mixed = jnp.zeros((CHUNK, GMLP_WIDTH), f32)
        for g in range(GMLP_GROUPS):
            wg = jnp.where(tc <= tr, wsp_ref[g], 0.0).astype(bf16)
            mg = jnp.dot(wg, vc, preferred_element_type=f32)
            mixed = jnp.where(lane_g == g, mg, mixed)
        mixed_chunks.append(mixed + bsp_ref[...])
    o_gmlp = u * jnp.concatenate(mixed_chunks, axis=0)
    cv = cv_ref[...]
    c_in = cv[:, :CONV_WIDTH]
    c_b = cv[:, CONV_WIDTH:2 * CONV_WIDTH]
    c_c = cv[:, 2 * CONV_WIDTH:]
    z = c_c * c_in
    cvp = cvp_ref[...]
    zp = cvp[:, 2 * CONV_WIDTH:] * cvp[:, :CONV_WIDTH]
    zp = jnp.where(i % blocks_per_seq == 0, jnp.zeros_like(zp), zp)
    rowi = lax.broadcasted_iota(jnp.int32, (tb, CONV_WIDTH), 0)
    z1 = pltpu.roll(z, 1, axis=0)
    z1 = jnp.where(rowi == 0, zp[7:8, :], z1)
    z2 = pltpu.roll(z, 2, axis=0)
    z2 = jnp.where(rowi == 0, zp[6:7, :], z2)
    z2 = jnp.where(rowi == 1, zp[7:8, :], z2)
    cw = cw_ref[...]
    o_conv = c_b * (cw[0:1, :] * z2 + cw[1:2, :] * z1 + cw[2:3, :] * z)
    mg_ = mg_ref[...]
    o_att = oatt_ref[0].T
    mixed_all = jnp.concatenate([
        _rms(o_att, mg_[:, :ATT_WIDTH]),
        _rms(o_gmlp, mg_[:, ATT_WIDTH:ATT_WIDTH + GMLP_WIDTH]),
        _rms(o_conv, mg_[:, ATT_WIDTH + GMLP_WIDTH:])], axis=-1).astype(bf16)
    xn = x_ref[...] + jnp.dot(mixed_all, wout_ref[...], preferred_element_type=f32)
    xo_ref[...] = xn
    h2_ref[...] = _rms(xn, n2g_ref[...]).T.astype(bf16)


def _mix_out(oatt_t, gm, cv, x2, sgug, wsp, bspf, cw, mg, wout, n2g, S, tb):
    T = x2.shape[0]
    bps = S // tb
    const2 = lambda i: (0, 0)
    row = lambda i: (i, 0)
    return pl.pallas_call(
        functools.partial(_mixout_kernel, tb=tb, blocks_per_seq=bps),
        grid=(T // tb,),
        in_specs=[pl.BlockSpec((1, ATT_WIDTH, tb), lambda i: (i // bps, 0, i % bps)),
                  pl.BlockSpec((tb, 2 * GMLP_WIDTH), row),
                  pl.BlockSpec((tb, 3 * CONV_WIDTH), row),
                  pl.BlockSpec((8, 3 * CONV_WIDTH),
                               lambda i: (jnp.maximum(i * (tb // 8) - 1, 0), 0)),
                  pl.BlockSpec((tb, D_MODEL), row),
                  pl.BlockSpec((1, GMLP_WIDTH), const2),
                  pl.BlockSpec((GMLP_GROUPS, CHUNK, CHUNK), lambda i: (0, 0, 0)),
                  pl.BlockSpec((CHUNK, GMLP_WIDTH), const2),
                  pl.BlockSpec((8, CONV_WIDTH), const2),
                  pl.BlockSpec((1, MIX_WIDTH), const2),
                  pl.BlockSpec((MIX_WIDTH, D_MODEL), const2),
                  pl.BlockSpec((1, D_MODEL), const2)],
        out_specs=[pl.BlockSpec((tb, D_MODEL), row),
                   pl.BlockSpec((D_MODEL, tb), lambda i: (0, i))],
        out_shape=[jax.ShapeDtypeStruct((T, D_MODEL), f32),
                   jax.ShapeDtypeStruct((D_MODEL, T), bf16)],
        compiler_params=_cparams(("parallel",)),
        name="mix_out",
    )(oatt_t, gm, cv, cv, x2, sgug, wsp, bspf, cw, mg, wout, n2g)


def _colmax(x):
    return jnp.max(x, axis=0, keepdims=True)


def _colsum(x):
    return jnp.sum(x, axis=0, keepdims=True)


def _tree(op, xs):
    while len(xs) > 1:
        xs = [op(xs[i], xs[i + 1]) if i + 1 < len(xs) else xs[i] for i in range(0, len(xs), 2)]
    return xs[0]


def _top16(scores, vals_refs):
    scores = list(scores)
    for a in range(PEER_TOPK):
        for i, s in enumerate(scores):
            m = _colmax(s)
            scores[i] = jnp.where(s == m, RANK_MARK * (1.0 + a * RANK_STEP), s)
            vals_refs[i][a:a + 1, :] = m
    ranks = []
    for s in scores:
        rank = jnp.round((s - RANK_MARK) * (1.0 / (RANK_MARK * RANK_STEP)))
        ranks.append(jnp.where(s <= RANK_MARK, rank, float(PEER_TOPK)))
    return ranks


def _select_tables(s1, s2, v1_ref, v2_ref):
    r1, r2 = _top16([s1, s2], [v1_ref, v2_ref])
    v1 = v1_ref[...]
    v2 = v2_ref[...]
    cands = [v1[0:1] + v2[0:8]]
    cands += [v1[a:a + 1] + v2[0:8] for a in range(1, 8)]
    cands += [v1[0:1] + v2[8:16], v1[8:16] + v2[0:1]]
    cmax = v1[0:1] + v2[0:1]
    work = list(cands)
    tau = cmax
    cnt = jnp.zeros_like(cmax)
    for _ in range(PEER_TOPK):
        m = _colmax(_tree(jnp.maximum, work))
        tau = jnp.where(cnt < float(PEER_TOPK), m, tau)
        hits = [w == m for w in work]
        cnt = cnt + _colsum(_tree(jnp.add, [jnp.where(h, 1.0, 0.0) for h in hits]))
        work = [jnp.where(h, -jnp.inf, w) for h, w in zip(hits, work)]
    sel = [c >= tau for c in cands]
    zsum = _colsum(_tree(jnp.add, [jnp.where(s_, jnp.exp(c - cmax), 0.0)
                                   for s_, c in zip(sel, cands)]))
    ones = [jnp.where(s_, 1.0, 0.0) for s_ in sel]
    n_lo = [_colsum(ones[0] + ones[8])] + [_colsum(o) for o in ones[1:8]]
    n_hi = ones[9]
    n1 = jnp.zeros(r1.shape, f32)
    for a in range(8):
        n1 = jnp.where(r1 == float(a), n_lo[a], n1)
    for a in range(8, PEER_TOPK):
        n1 = jnp.where(r1 == float(a), n_hi[a - 8:a - 7], n1)
    e2 = jnp.exp(s2 - v2[0:1])
    c1 = jnp.exp(s1 - v1[0:1]) / zsum
    return r2.astype(bf16), e2.astype(bf16), n1, c1


def _peerprep_kernel(h2_ref, wq_ref, keys_ref, r2_ref, e2_ref, n1_ref, c1_ref,
                     v1_ref, v2_ref):
    qT = jnp.dot(wq_ref[...], h2_ref[...], preferred_element_type=f32)
    qb = qT.astype(bf16)
    s1 = jnp.dot(keys_ref[0], qb[:D_HALF], preferred_element_type=f32)
    s2 = jnp.dot(keys_ref[1], qb[D_HALF:], preferred_element_type=f32)
    lpc = SLAB // LANES
    for c in range(r2_ref.shape[1]):
        r2s, e2s = [], []
        for k in range(lpc):
            t = c * lpc + k
            ls = slice(t * LANES, (t + 1) * LANES)
            r2, e2, n1, c1 = _select_tables(s1[:, ls], s2[:, ls], v1_ref, v2_ref)
            r2s.append(r2)
            e2s.append(e2)
            n1_ref[0, t] = n1
            c1_ref[0, t] = c1
        r2_ref[0, c] = jnp.concatenate(r2s, axis=1)
        e2_ref[0, c] = jnp.concatenate(e2s, axis=1)


def _peer_prep(h2t, wqT, keys, tb):
    T = h2t.shape[1]
    spec = pl.BlockSpec((1, tb // SLAB, N_KEYS, SLAB), lambda t, h: (h, t, 0, 0))
    shp = (PEER_HEADS, T // SLAB, N_KEYS, SLAB)
    rspec = pl.BlockSpec((1, tb // LANES, N_KEYS, LANES), lambda t, h: (h, t, 0, 0))
    rshp = (PEER_HEADS, T // LANES, N_KEYS, LANES)
    return pl.pallas_call(
        _peerprep_kernel,
        grid=(T // tb, PEER_HEADS),
        in_specs=[pl.BlockSpec((D_MODEL, tb), lambda t, h: (0, t)),
                  pl.BlockSpec((D_KEY, D_MODEL), lambda t, h: (h, 0)),
                  pl.BlockSpec((2, N_KEYS, D_HALF), lambda t, h: (0, 0, 0))],
        out_specs=[spec, spec, rspec, rspec],
        out_shape=[jax.ShapeDtypeStruct(shp, bf16), jax.ShapeDtypeStruct(shp, bf16),
                   jax.ShapeDtypeStruct(rshp, f32), jax.ShapeDtypeStruct(rshp, f32)],
        scratch_shapes=[pltpu.VMEM((PEER_TOPK, LANES), f32), pltpu.VMEM((PEER_TOPK, LANES), f32)],
        compiler_params=_cparams(("parallel", "arbitrary")),
        name="peer_prep",
    )(h2t, wqT, keys)


def _peermain_kernel(h2_ref, u_ref, vt_ref, r2_ref, e2_ref, n1_ref, c1_ref, x_ref, fg_ref,
                     o_ref, acc_ref, *, rows, final):
    e = pl.program_id(1)
    tb = h2_ref.shape[1]
    nc = tb // SLAB
    lpc = SLAB // LANES

    @pl.when(e == 0)
    def _():
        acc_ref[...] = jnp.zeros_like(acc_ref)

    base = e * rows

    def row_tile(ref, h, r, c):
        row = jnp.concatenate(
            [jnp.broadcast_to(ref[h, c * lpc + k, pl.ds(base + r, 1), :], (BF16_ROWS, LANES))
             for k in range(lpc)], axis=1).astype(bf16)
        return jnp.tile(row, (N_KEYS // BF16_ROWS, 1))

    zero = jnp.zeros((N_KEYS, SLAB), bf16)

    ub = UNIT_ROWS * N_KEYS
    units = [(c, u) for c in range(nc) for u in range(rows // UNIT_ROWS)]

    def activations(unit):
        c, u = unit
        return jnp.dot(u_ref[u * ub:(u + 1) * ub, :], h2_ref[:, c * SLAB:(c + 1) * SLAB],
                       preferred_element_type=f32)

    def gates(unit):
        c, u = unit
        out = []
        for r in range(u * UNIT_ROWS, (u + 1) * UNIT_ROWS):
            g = None
            for h in range(PEER_HEADS):
                term = jnp.where(r2_ref[h, c] < row_tile(n1_ref, h, r, c), e2_ref[h, c], zero)
                term = term * row_tile(c1_ref, h, r, c)
                g = term if g is None else g + term
            out.append(g)
        return jnp.concatenate(out, axis=0)

    upc = rows // UNIT_ROWS

    def accumulate(c, w):
        cs = slice(c * SLAB, (c + 1) * SLAB)
        acc_ref[:, cs] += jnp.dot(vt_ref[...], w, preferred_element_type=f32)

    a, g, w = {}, {}, {}
    for t in range(len(units) + 3):
        if t < len(units):
            a[t] = activations(units[t])
        if 0 <= t - 1 < len(units):
            g[t - 1] = gates(units[t - 1])
        if 0 <= t - 2 < len(units):
            w[t - 2] = _gelu(a.pop(t - 2).astype(bf16)) * g.pop(t - 2)
        if 0 <= t - 3 < len(units) and (t - 3) % upc == upc - 1:
            c = (t - 3) // upc
            accumulate(c, jnp.concatenate([w.pop(c * upc + k) for k in range(upc)], axis=0))

    @pl.when(e == pl.num_programs(1) - 1)
    def _():
        y = x_ref[...] + acc_ref[...].T
        o_ref[...] = _rms(y, fg_ref[...]) if final else y


def _peer_main(h2t, u, vt, r2, e2, n1, c1, x2, fg, layer, tb, rows, final):
    T = h2t.shape[1]
    eb = rows * N_KEYS
    sel = pl.BlockSpec((PEER_HEADS, tb // SLAB, N_KEYS, SLAB), lambda t, e: (0, t, 0, 0))
    rsel = pl.BlockSpec((PEER_HEADS, tb // LANES, N_KEYS, LANES), lambda t, e: (0, t, 0, 0))
    return pl.pallas_call(
        functools.partial(_peermain_kernel, rows=rows, final=final),
        grid=(T // tb, N_KEYS // rows),
        in_specs=[pl.BlockSpec((D_MODEL, tb), lambda t, e: (0, t)),
                  pl.BlockSpec((None, eb, D_MODEL), lambda t, e: (layer, e, 0)),
                  pl.BlockSpec((None, D_MODEL, eb), lambda t, e: (layer, 0, e)),
                  sel, sel, rsel, rsel,
                  pl.BlockSpec((tb, D_MODEL), lambda t, e: (t, 0)),
                  pl.BlockSpec((1, D_MODEL), lambda t, e: (0, 0))],
        out_specs=pl.BlockSpec((tb, D_MODEL), lambda t, e: (t, 0)),
        out_shape=jax.ShapeDtypeStruct((T, D_MODEL), f32),
        scratch_shapes=[pltpu.VMEM((D_MODEL, tb), f32)],
        compiler_params=_cparams(("parallel", "arbitrary")),
        name="peer_main",
    )(h2t, u, vt, r2, e2, n1, c1, x2, fg)


def _block_sizes(S):
    return dict(tb_in=min(1024, S), tc=min(512, S), tq=min(512, S), heads_per_step=8,
                tb_mix=min(512, S), tb_prep=min(2048, S), tb_peer=min(512, S), rows=16)


def kernel(x, norm1_g, w_in, b_forget, sgu_norm_g, w_spatial, b_spatial, conv_w,
           mix_norm_g, w_out, norm2_g, w_query, sub_keys, expert_u, expert_v, final_g):
    B, S, D = x.shape
    T = B * S
    depth = w_in.shape[0]
    bs = _block_sizes(S)
    x2 = x.reshape(T, D)
    u_all = expert_u.astype(bf16)
    vt_all = expert_v.transpose(0, 2, 1).astype(bf16)
    for l in range(depth):
        w_main, wf = _in_proj_weights(w_in[l])
        bfp = jnp.pad(b_forget[l], (0, LANES - ATT_HEADS)).reshape(1, LANES)
        qa, kp, vt, gm, cv, logf, nrm = _in_proj(x2, norm1_g[l].reshape(1, D), w_main, wf, bfp,
                                                 B, S, bs["tb_in"])
        ka = _cumsum(logf, kp, bs["tc"])
        jstart = _first_key_block(nrm, ka, bs["tq"], bs["heads_per_step"])
        oatt_t = _attention(jstart, qa, ka, vt, bs["tq"], bs["heads_per_step"])
        bspf = jnp.repeat(b_spatial[l].T, GROUP_DIM, axis=1)
        cw = jnp.pad(conv_w[l], ((0, 8 - CONV_K), (0, 0)))
        x2, h2t = _mix_out(oatt_t, gm, cv, x2, sgu_norm_g[l].reshape(1, GMLP_WIDTH),
                          w_spatial[l], bspf, cw, mix_norm_g[l].reshape(1, MIX_WIDTH),
                          w_out[l].astype(bf16), norm2_g[l].reshape(1, D), S, bs["tb_mix"])
        r2, e2, n1, c1 = _peer_prep(h2t, w_query[l].T.astype(bf16), sub_keys[l].astype(bf16),
                                    bs["tb_prep"])
        x2 = _peer_main(h2t, u_all, vt_all, r2, e2, n1, c1, x2, final_g.reshape(1, D), l,
                        bs["tb_peer"], bs["rows"], final=(l == depth - 1))
    return x2.reshape(B, S, D)
```

```python
import functools
import math

import jax
import jax.numpy as jnp
from jax import lax
from jax.experimental import pallas as pl
from jax.experimental.pallas import tpu as pltpu

f32 = jnp.float32
bf16 = jnp.bfloat16

D_MODEL = 1024
ATT_WIDTH = 512
ATT_HEADS = 8
HEAD_DIM = 64
GMLP_WIDTH = 256
GMLP_GROUPS = 4
GROUP_DIM = 64
CHUNK = 128
CONV_WIDTH = 256
CONV_K = 3
MIX_WIDTH = 1024
PEER_HEADS = 8
N_KEYS = 128
PEER_TOPK = 16
D_KEY = 256
D_HALF = 128
EPS = 1e-6
LANES = 128
BF16_ROWS = 16
SLAB = 256
UNIT_ROWS = 8
NEG_BIG = -1e30
LOG2E = math.log2(math.e)
GELU_K0 = -2.0 * math.sqrt(2.0 / math.pi)
GELU_K1 = GELU_K0 * 0.044715
F_PARTS = 3
V_ROWS = 80
UNDERFLOW_LOG2 = 160.0
NORM_SLACK = 1.02
RANK_MARK = -3.0e38
RANK_STEP = 2.0 ** -10
CAND_PACKING = (((2, 5), (4, 3)), ((3, 4), (5, 2), (6, 2)), ((7, 2),))

VMEM_LIMIT = 56 * 1024 * 1024

_NT = (((1,), (1,)), ((), ()))


def _cparams(sem, flags=None):
    return pltpu.CompilerParams(dimension_semantics=sem, vmem_limit_bytes=VMEM_LIMIT, flags=flags)


def _gelu(x):
    return x / (1.0 + jnp.exp(x * (GELU_K0 + GELU_K1 * (x * x))))


def _rms(x, g):
    return x * lax.rsqrt(jnp.mean(x * x, axis=-1, keepdims=True) + EPS) * g


def _split3(x):
    hi = x.astype(bf16)
    r = x - hi.astype(f32)
    mid = r.astype(bf16)
    lo = (r - mid.astype(f32)).astype(bf16)
    return hi, mid, lo


def _inproj_kernel(x_ref, g_ref, w_ref, wf_ref, bf_ref,
                   qa_ref, kp_ref, vt_ref, gm_ref, cv_ref, logf_ref, nrm_ref):
    tb = x_ref.shape[0]
    h = _rms(x_ref[...], g_ref[...])
    hb = h.astype(bf16)
    z = jnp.dot(hb, w_ref[...], preferred_element_type=f32)
    hw = ATT_HEADS * LANES
    lane = lax.broadcasted_iota(jnp.int32, (1, hw), 1) % LANES
    fcol = (lane >= HEAD_DIM) & (lane < HEAD_DIM + F_PARTS)
    zq = z[:, :hw] * (HEAD_DIM ** -0.5 * LOG2E)
    zk = z[:, hw:2 * hw]
    qa = jnp.where(fcol, -1.0, zq).astype(bf16)
    kp = zk.astype(bf16)
    vT = z[:, 2 * hw:2 * hw + ATT_WIDTH].T
    tail = jnp.where(lax.broadcasted_iota(jnp.int32, (V_ROWS - HEAD_DIM, tb), 0) == 0,
                     1.0, 0.0).astype(bf16)
    for hd in range(ATT_HEADS):
        qa_ref[0, hd] = qa[:, hd * LANES:(hd + 1) * LANES]
        kp_ref[0, hd] = kp[:, hd * LANES:(hd + 1) * LANES]
        vt_ref[0, hd, :HEAD_DIM, :] = vT[hd * HEAD_DIM:(hd + 1) * HEAD_DIM, :].astype(bf16)
        vt_ref[0, hd, HEAD_DIM:, :] = tail
    c0 = 2 * hw + ATT_WIDTH
    gm_ref[...] = z[:, c0:c0 + 2 * GMLP_WIDTH]
    cv_ref[...] = z[:, c0 + 2 * GMLP_WIDTH:]
    r = lax.broadcasted_iota(jnp.int32, (hw, LANES), 0) // LANES
    c = lax.broadcasted_iota(jnp.int32, (hw, LANES), 1)
    ind_q = jnp.where(r == c, 1.0, 0.0).astype(bf16)
    ind_k = jnp.where(r + ATT_HEADS == c, 1.0, 0.0).astype(bf16)
    nrm_ref[...] = (jnp.dot((zq * zq).astype(bf16), ind_q, preferred_element_type=f32)
                    + jnp.dot((zk * zk).astype(bf16), ind_k, preferred_element_type=f32))
    hl = (h - hb.astype(f32)).astype(bf16)
    wf = wf_ref[...]
    wfh = wf.astype(bf16)
    wfl = (wf - wfh.astype(f32)).astype(bf16)
    zf = (jnp.dot(hb, wfh, preferred_element_type=f32)
          + jnp.dot(hb, wfl, preferred_element_type=f32)
          + jnp.dot(hl, wfh, preferred_element_type=f32)) + bf_ref[...]
    logf_ref[...] = jnp.minimum(zf, 0.0) - jnp.log1p(jnp.exp(-jnp.abs(zf)))


def _in_proj(x2, g, w, wf, bfp, B, S, tb):
    T = x2.shape[0]
    bps = S // tb
    ncol = w.shape[1]
    const = lambda i: (0, 0)
    row = lambda i: (i, 0)
    head_spec = pl.BlockSpec((1, ATT_HEADS, tb, LANES), lambda i: (i // bps, 0, i % bps, 0))
    head_shape = jax.ShapeDtypeStruct((B, ATT_HEADS, S, LANES), bf16)
    return pl.pallas_call(
        _inproj_kernel,
        grid=(T // tb,),
        in_specs=[pl.BlockSpec((tb, D_MODEL), row),
                  pl.BlockSpec((1, D_MODEL), const),
                  pl.BlockSpec((D_MODEL, ncol), const),
                  pl.BlockSpec((D_MODEL, LANES), const),
                  pl.BlockSpec((1, LANES), const)],
        out_specs=[head_spec, head_spec,
                   pl.BlockSpec((1, ATT_HEADS, V_ROWS, tb), lambda i: (i // bps, 0, 0, i % bps)),
                   pl.BlockSpec((tb, 2 * GMLP_WIDTH), row),
                   pl.BlockSpec((tb, 3 * CONV_WIDTH), row),
                   pl.BlockSpec((tb, LANES), row),
                   pl.BlockSpec((tb, LANES), row)],
        out_shape=[head_shape, head_shape,
                   jax.ShapeDtypeStruct((B, ATT_HEADS, V_ROWS, S), bf16),
                   jax.ShapeDtypeStruct((T, 2 * GMLP_WIDTH), f32),
                   jax.ShapeDtypeStruct((T, 3 * CONV_WIDTH), f32),
                   jax.ShapeDtypeStruct((T, LANES), f32),
                   jax.ShapeDtypeStruct((T, LANES), f32)],
        compiler_params=_cparams(("parallel",)),
        name="in_proj",
    )(x2, g, w, wf, bfp)


def _in_proj_weights(wl):
    f0 = 3 * ATT_WIDTH

    def per_head(cols):
        w3 = cols.reshape(D_MODEL, ATT_HEADS, HEAD_DIM)
        return jnp.pad(w3, ((0, 0), (0, 0), (0, LANES - HEAD_DIM))).reshape(D_MODEL, -1)

    w_main = jnp.concatenate([per_head(wl[:, :ATT_WIDTH]), per_head(wl[:, ATT_WIDTH:2 * ATT_WIDTH]),
                              wl[:, 2 * ATT_WIDTH:f0], wl[:, f0 + ATT_HEADS:]], axis=1)
    wf = jnp.pad(wl[:, f0:f0 + ATT_HEADS], ((0, 0), (0, LANES - ATT_HEADS)))
    return w_main.astype(bf16), wf


def _cumsum_kernel(x_ref, kp_ref, ka_ref, carry_ref, *, tc):
    @pl.when(pl.program_id(1) == 0)
    def _():
        carry_ref[...] = jnp.zeros_like(carry_ref)

    row = lax.broadcasted_iota(jnp.int32, (tc, tc), 0)
    col = lax.broadcasted_iota(jnp.int32, (tc, tc), 1)
    tri = jnp.where(col <= row, 1.0, 0.0).astype(bf16)
    hi, mid, lo = _split3(x_ref[...])
    c = (jnp.dot(tri, hi, preferred_element_type=f32)
         + jnp.dot(tri, mid, preferred_element_type=f32)
         + jnp.dot(tri, lo, preferred_element_type=f32)) + carry_ref[...]
    carry_ref[...] = c[tc - 1:tc, :]
    hw = ATT_HEADS * LANES
    src = lax.broadcasted_iota(jnp.int32, (LANES, hw), 0)
    dst = lax.broadcasted_iota(jnp.int32, (LANES, hw), 1)
    fcols = jnp.zeros((tc, hw), f32)
    for p, part in enumerate(_split3(c * LOG2E)):
        place = jnp.where(dst == src * LANES + HEAD_DIM + p, 1.0, 0.0).astype(bf16)
        fcols = fcols + jnp.dot(part, place, preferred_element_type=f32)
    for hd in range(ATT_HEADS):
        ka_ref[0, hd] = (kp_ref[0, hd].astype(f32)
                         + fcols[:, hd * LANES:(hd + 1) * LANES]).astype(bf16)


def _cumsum(logf, kp, tc):
    B, H, S, _ = kp.shape
    nb = S // tc
    head_spec = pl.BlockSpec((1, H, tc, LANES), lambda b, j: (b, 0, j, 0))
    return pl.pallas_call(
        functools.partial(_cumsum_kernel, tc=tc),
        grid=(B, nb),
        in_specs=[pl.BlockSpec((tc, LANES), lambda b, j: (b * nb + j, 0)), head_spec],
        out_specs=head_spec,
        out_shape=jax.ShapeDtypeStruct(kp.shape, bf16),
        scratch_shapes=[pltpu.VMEM((1, LANES), f32)],
        compiler_params=_cparams(("arbitrary", "arbitrary")),
        name="cumsum_logf",
    )(logf, kp)


def _attn_kernel(jstart_ref, qa_ref, ka_ref, vt_ref, o_ref, acc_ref, m_ref, *, tq, hg):
    i = pl.program_id(2)
    m_ref[...] = jnp.full(m_ref.shape, NEG_BIG, f32)
    acc_ref[...] = jnp.zeros(acc_ref.shape, f32)

    def step(j, masked):
        start = pl.multiple_of(j * tq, tq)

        def scores(h):
            k = ka_ref[0, h, pl.ds(start, tq), :]
            s = lax.dot_general(k, qa_ref[0, h], _NT, preferred_element_type=f32)
            if masked:
                r = lax.broadcasted_iota(jnp.int32, (tq, tq), 0)
                c = lax.broadcasted_iota(jnp.int32, (tq, tq), 1)
                s = jnp.where(r <= c, s, NEG_BIG)
            return s

        def new_max(h, s):
            return jnp.maximum(m_ref[h], jnp.max(s, axis=0, keepdims=True))

        def weights(s, m_new):
            return jnp.exp2(s - m_new).astype(bf16)

        def accumulate(h, p, m_new):
            pv = jnp.dot(vt_ref[0, h, :, pl.ds(start, tq)], p, preferred_element_type=f32)
            acc_ref[h] = jnp.exp2(m_ref[h] - m_new) * acc_ref[h] + pv
            m_ref[h] = m_new

        s, mx, p = {}, {}, {}
        for t in range(hg + 3):
            if t < hg:
                s[t] = scores(t)
            if 0 <= t - 1 < hg:
                mx[t - 1] = new_max(t - 1, s[t - 1])
            if 0 <= t - 2 < hg:
                p[t - 2] = weights(s.pop(t - 2), mx[t - 2])
            if 0 <= t - 3 < hg:
                accumulate(t - 3, p.pop(t - 3), mx.pop(t - 3))

    def body(j, carry):
        step(j, False)
        return carry

    lax.fori_loop(jstart_ref[pl.program_id(0), pl.program_id(1), i], i, body, 0)
    step(i, True)
    for h in range(hg):
        o_ref[0, h * HEAD_DIM:(h + 1) * HEAD_DIM, :] = (
            acc_ref[h, :HEAD_DIM, :] / acc_ref[h, HEAD_DIM:HEAD_DIM + 1, :])


def _attention(jstart, qa, ka, vt, tq, hg):
    B, H, S, _ = qa.shape
    return pl.pallas_call(
        functools.partial(_attn_kernel, tq=tq, hg=hg),
        grid_spec=pltpu.PrefetchScalarGridSpec(
            num_scalar_prefetch=1,
            grid=(B, H // hg, S // tq),
            in_specs=[pl.BlockSpec((1, hg, tq, LANES), lambda b, g, i, js: (b, g, i, 0)),
                      pl.BlockSpec((1, hg, S, LANES), lambda b, g, i, js: (b, g, 0, 0),
                                   pipeline_mode=pl.Buffered(1)),
                      pl.BlockSpec((1, hg, V_ROWS, S), lambda b, g, i, js: (b, g, 0, 0),
                                   pipeline_mode=pl.Buffered(1))],
            out_specs=pl.BlockSpec((1, hg * HEAD_DIM, tq), lambda b, g, i, js: (b, g, i)),
            scratch_shapes=[pltpu.VMEM((hg, V_ROWS, tq), f32), pltpu.VMEM((hg, 1, tq), f32)]),
        out_shape=jax.ShapeDtypeStruct((B, H * HEAD_DIM, S), f32),
        compiler_params=_cparams(("parallel", "parallel", "arbitrary")),
        name="fox_attention",
    )(jstart, qa, ka, vt)


def _first_key_block(nrm, ka, tq, hg):
    B, H, S, _ = ka.shape
    nq = S // tq
    n2 = jnp.max(nrm.reshape(B, nq, tq, LANES), axis=2)
    qmax = jnp.sqrt(n2[:, :, :H]).transpose(0, 2, 1) * NORM_SLACK
    kmax = jnp.sqrt(n2[:, :, H:2 * H]).transpose(0, 2, 1) * NORM_SLACK
    blk = jnp.arange(nq, dtype=jnp.int32)
    earlier = blk[None, :] <= blk[:, None]
    kcum = jnp.max(jnp.where(earlier[None, None], kmax[:, :, None, :], 0.0), axis=3)
    fl = ka[:, :, :, HEAD_DIM:HEAD_DIM + F_PARTS].astype(f32).sum(-1)
    f_end = fl[:, :, tq - 1::tq]
    f_start = fl[:, :, ::tq]
    upper = qmax[:, :, :, None] * kcum[:, :, None, :] - f_end[:, :, None, :]
    lower = -qmax * kmax - f_start
    skip = upper < lower[:, :, :, None] - UNDERFLOW_LOG2
    lead = jnp.min(jnp.where(skip, nq, blk[None, None, None, :]), axis=3)
    lead = jnp.minimum(lead, blk[None, None, :])
    return lead.reshape(B, H // hg, hg, nq).min(axis=2).astype(jnp.int32)


def _mixout_kernel(oatt_ref, gm_ref, cv_ref, cvp_ref, x_ref, sgug_ref, wsp_ref, bsp_ref,
                   cw_ref, mg_ref, wout_ref, n2g_ref, xo_ref, h2_ref, *, tb, blocks_per_seq):
    i = pl.program_id(0)
    gm = gm_ref[...]
    u = _gelu(gm[:, :GMLP_WIDTH])
    v = _gelu(gm[:, GMLP_WIDTH:])
    gr = lax.broadcasted_iota(jnp.int32, (GMLP_WIDTH, GMLP_WIDTH), 0) // GROUP_DIM
    gc = lax.broadcasted_iota(jnp.int32, (GMLP_WIDTH, GMLP_WIDTH), 1) // GROUP_DIM
    grp = jnp.where(gr == gc, 1.0 / GROUP_DIM, 0.0).astype(bf16)
    vh, vm, vl = _split3(v * v)
    ms = (jnp.dot(vh, grp, preferred_element_type=f32)
          + jnp.dot(vm, grp, preferred_element_type=f32)
          + jnp.dot(vl, grp, preferred_element_type=f32))
    vn = (v * lax.rsqrt(ms + EPS) * sgug_ref[...]).astype(bf16)
    tr = lax.broadcasted_iota(jnp.int32, (CHUNK, CHUNK), 0)
    tc = lax.broadcasted_iota(jnp.int32, (CHUNK, CHUNK), 1)
    lane_g = lax.broadcasted_iota(jnp.int32, (CHUNK, GMLP_WIDTH), 1) // GROUP_DIM
    mixed_chunks = []
    for c in range(tb // CHUNK):
        vc = vn[c * CHUNK:(c + 1) * CHUNK, :]
        mixed = jnp.zeros((CHUNK, GMLP_WIDTH), f32)
        for g in range(GMLP_GROUPS):
            wg = jnp.where(tc <= tr, wsp_ref[g], 0.0).astype(bf16)
            mg = jnp.dot(wg, vc, preferred_element_type=f32)
            mixed = jnp.where(lane_g == g, mg, mixed)
        mixed_chunks.append(mixed + bsp_ref[...])
    o_gmlp = u * jnp.concatenate(mixed_chunks, axis=0)
    cv = cv_ref[...]
    c_in = cv[:, :CONV_WIDTH]
    c_b = cv[:, CONV_WIDTH:2 * CONV_WIDTH]
    c_c = cv[:, 2 * CONV_WIDTH:]
    z = c_c * c_in
    cvp = cvp_ref[...]
    zp = cvp[:, 2 * CONV_WIDTH:] * cvp[:, :CONV_WIDTH]
    zp = jnp.where(i % blocks_per_seq == 0, jnp.zeros_like(zp), zp)
    rowi = lax.broadcasted_iota(jnp.int32, (tb, CONV_WIDTH), 0)
    z1 = pltpu.roll(z, 1, axis=0)
    z1 = jnp.where(rowi == 0, zp[7:8, :], z1)
    z2 = pltpu.roll(z, 2, axis=0)
    z2 = jnp.where(rowi == 0, zp[6:7, :], z2)
    z2 = jnp.where(rowi == 1, zp[7:8, :], z2)
    cw = cw_ref[...]
    o_conv = c_b * (cw[0:1, :] * z2 + cw[1:2, :] * z1 + cw[2:3, :] * z)
    mg_ = mg_ref[...]
    o_att = oatt_ref[0].T
    mixed_all = jnp.concatenate([
        _rms(o_att, mg_[:, :ATT_WIDTH]),
        _rms(o_gmlp, mg_[:, ATT_WIDTH:ATT_WIDTH + GMLP_WIDTH]),
        _rms(o_conv, mg_[:, ATT_WIDTH + GMLP_WIDTH:])], axis=-1).astype(bf16)
    xn = x_ref[...] + jnp.dot(mixed_all, wout_ref[...], preferred_element_type=f32)
    xo_ref[...] = xn
    h2_ref[...] = _rms(xn, n2g_ref[...]).T.astype(bf16)


def _mix_out(oatt_t, gm, cv, x2, sgug, wsp, bspf, cw, mg, wout, n2g, S, tb):
    T = x2.shape[0]
    bps = S // tb
    const2 = lambda i: (0, 0)
    row = lambda i: (i, 0)
    return pl.pallas_call(
        functools.partial(_mixout_kernel, tb=tb, blocks_per_seq=bps),
        grid=(T // tb,),
        in_specs=[pl.BlockSpec((1, ATT_WIDTH, tb), lambda i: (i // bps, 0, i % bps)),
                  pl.BlockSpec((tb, 2 * GMLP_WIDTH), row),
                  pl.BlockSpec((tb, 3 * CONV_WIDTH), row),
                  pl.BlockSpec((8, 3 * CONV_WIDTH),
                               lambda i: (jnp.maximum(i * (tb // 8) - 1, 0), 0)),
                  pl.BlockSpec((tb, D_MODEL), row),
                  pl.BlockSpec((1, GMLP_WIDTH), const2),
                  pl.BlockSpec((GMLP_GROUPS, CHUNK, CHUNK), lambda i: (0, 0, 0)),
                  pl.BlockSpec((CHUNK, GMLP_WIDTH), const2),
                  pl.BlockSpec((8, CONV_WIDTH), const2),
                  pl.BlockSpec((1, MIX_WIDTH), const2),
                  pl.BlockSpec((MIX_WIDTH, D_MODEL), const2),
                  pl.BlockSpec((1, D_MODEL), const2)],
        out_specs=[pl.BlockSpec((tb, D_MODEL), row),
                   pl.BlockSpec((D_MODEL, tb), lambda i: (0, i))],
        out_shape=[jax.ShapeDtypeStruct((T, D_MODEL), f32),
                   jax.ShapeDtypeStruct((D_MODEL, T), bf16)],
        compiler_params=_cparams(("parallel",)),
        name="mix_out",
    )(oatt_t, gm, cv, cv, x2, sgug, wsp, bspf, cw, mg, wout, n2g)


def _colmax(x):
    return jnp.max(x, axis=0, keepdims=True)


def _colsum(x):
    return jnp.sum(x, axis=0, keepdims=True)


def _tree(op, xs):
    while len(xs) > 1:
        xs = [op(xs[i], xs[i + 1]) if i + 1 < len(xs) else xs[i] for i in range(0, len(xs), 2)]
    return xs[0]


def _top16(scores, vals_refs):
    scores = list(scores)
    for a in range(PEER_TOPK):
        for i, s in enumerate(scores):
            m = _colmax(s)
            scores[i] = jnp.where(s == m, RANK_MARK * (1.0 + a * RANK_STEP), s)
            vals_refs[i][a:a + 1, :] = m
    ranks = []
    for s in scores:
        rank = jnp.round((s - RANK_MARK) * (1.0 / (RANK_MARK * RANK_STEP)))
        ranks.append(jnp.where(s <= RANK_MARK, rank, float(PEER_TOPK)))
    return ranks


def _select_tables(s1, s2, v1_ref, v2_ref):
    r1, r2 = _top16([s1, s2], [v1_ref, v2_ref])
    v1 = v1_ref[...]
    v2 = v2_ref[...]
    row8 = lax.broadcasted_iota(jnp.int32, (8, s1.shape[1]), 0)
    v2lo = v2[0:8]

    def packed(segments):
        arr, off = None, 0
        for a, nb in segments:
            src = v1[a:a + 1] + (v2lo if off == 0 else pltpu.roll(v2lo, off, axis=0))
            arr = src if arr is None else jnp.where(row8 >= off, src, arr)
            off += nb
        return arr if off == 8 else jnp.where(row8 >= off, -jnp.inf, arr)

    cands = [v1[0:1] + v2lo, v1[0:1] + v2[8:16], v1[1:2] + v2lo, v1[8:16] + v2[0:1]]
    cands += [packed(seg) for seg in CAND_PACKING]
    cmax = v1[0:1] + v2[0:1]
    work = list(cands)
    tau = cmax
    cnt = jnp.zeros_like(cmax)
    for _ in range(PEER_TOPK):
        m = _colmax(_tree(jnp.maximum, work))
        tau = jnp.where(cnt < float(PEER_TOPK), m, tau)
        hits = [w == m for w in work]
        cnt = cnt + _colsum(_tree(jnp.add, [jnp.where(h, 1.0, 0.0) for h in hits]))
        work = [jnp.where(h, -jnp.inf, w) for h, w in zip(hits, work)]
    sel = [c >= tau for c in cands]
    zsum = _colsum(_tree(jnp.add, [jnp.where(s_, jnp.exp(c - cmax), 0.0)
                                   for s_, c in zip(sel, cands)]))
    ones = [jnp.where(s_, 1.0, 0.0) for s_ in sel]
    n_lo = {0: _colsum(ones[0] + ones[1]), 1: _colsum(ones[2])}
    for o, segments in zip(ones[4:], CAND_PACKING):
        off = 0
        for a, nb in segments:
            inside = jnp.where(row8 >= off, jnp.where(row8 < off + nb, o, 0.0), 0.0)
            n_lo[a] = _colsum(inside)
            off += nb
    n_hi = ones[3]
    n1 = jnp.zeros(r1.shape, f32)
    for a in range(8):
        n1 = jnp.where(r1 == float(a), n_lo[a], n1)
    for a in range(8, PEER_TOPK):
        n1 = jnp.where(r1 == float(a), n_hi[a - 8:a - 7], n1)
    e2 = jnp.exp(s2 - v2[0:1])
    c1 = jnp.exp(s1 - v1[0:1]) / zsum
    return r2.astype(bf16), e2.astype(bf16), n1, c1


def _peerprep_kernel(h2_ref, wq_ref, keys_ref, r2_ref, e2_ref, n1_ref, c1_ref,
                     v1_ref, v2_ref):
    qT = jnp.dot(wq_ref[...], h2_ref[...], preferred_element_type=f32)
    qb = qT.astype(bf16)
    s1 = jnp.dot(keys_ref[0], qb[:D_HALF], preferred_element_type=f32)
    s2 = jnp.dot(keys_ref[1], qb[D_HALF:], preferred_element_type=f32)
    lpc = SLAB // LANES
    for c in range(r2_ref.shape[1]):
        r2s, e2s = [], []
        for k in range(lpc):
            t = c * lpc + k
            ls = slice(t * LANES, (t + 1) * LANES)
            r2, e2, n1, c1 = _select_tables(s1[:, ls], s2[:, ls], v1_ref, v2_ref)
            r2s.append(r2)
            e2s.append(e2)
            n1_ref[0, t] = n1
            c1_ref[0, t] = c1
        r2_ref[0, c] = jnp.concatenate(r2s, axis=1)
        e2_ref[0, c] = jnp.concatenate(e2s, axis=1)


def _peer_prep(h2t, wqT, keys, tb):
    T = h2t.shape[1]
    spec = pl.BlockSpec((1, tb // SLAB, N_KEYS, SLAB), lambda t, h: (h, t, 0, 0))
    shp = (PEER_HEADS, T // SLAB, N_KEYS, SLAB)
    rspec = pl.BlockSpec((1, tb // LANES, N_KEYS, LANES), lambda t, h: (h, t, 0, 0))
    rshp = (PEER_HEADS, T // LANES, N_KEYS, LANES)
    return pl.pallas_call(
        _peerprep_kernel,
        grid=(T // tb, PEER_HEADS),
        in_specs=[pl.BlockSpec((D_MODEL, tb), lambda t, h: (0, t)),
                  pl.BlockSpec((D_KEY, D_MODEL), lambda t, h: (h, 0)),
                  pl.BlockSpec((2, N_KEYS, D_HALF), lambda t, h: (0, 0, 0))],
        out_specs=[spec, spec, rspec, rspec],
        out_shape=[jax.ShapeDtypeStruct(shp, bf16), jax.ShapeDtypeStruct(shp, bf16),
                   jax.ShapeDtypeStruct(rshp, f32), jax.ShapeDtypeStruct(rshp, f32)],
        scratch_shapes=[pltpu.VMEM((PEER_TOPK, LANES), f32), pltpu.VMEM((PEER_TOPK, LANES), f32)],
        compiler_params=_cparams(("parallel", "arbitrary")),
        name="peer_prep",
    )(h2t, wqT, keys)


def _peermain_kernel(h2_ref, u_ref, vt_ref, r2_ref, e2_ref, n1_ref, c1_ref, x_ref, fg_ref,
                     o_ref, acc_ref, *, rows, final):
    e = pl.program_id(1)
    tb = h2_ref.shape[1]
    nc = tb // SLAB
    lpc = SLAB // LANES

    @pl.when(e == 0)
    def _():
        acc_ref[...] = jnp.zeros_like(acc_ref)

    base = e * rows

    def row_tile(ref, h, r, c):
        row = jnp.concatenate(
            [jnp.broadcast_to(ref[h, c * lpc + k, pl.ds(base + r, 1), :], (BF16_ROWS, LANES))
             for k in range(lpc)], axis=1).astype(bf16)
        return jnp.tile(row, (N_KEYS // BF16_ROWS, 1))

    zero = jnp.zeros((N_KEYS, SLAB), bf16)

    ub = UNIT_ROWS * N_KEYS
    units = [(c, u) for c in range(nc) for u in range(rows // UNIT_ROWS)]

    def activations(unit):
        c, u = unit
        return jnp.dot(u_ref[u * ub:(u + 1) * ub, :], h2_ref[:, c * SLAB:(c + 1) * SLAB],
                       preferred_element_type=f32)

    def gates(unit):
        c, u = unit
        out = []
        for r in range(u * UNIT_ROWS, (u + 1) * UNIT_ROWS):
            g = None
            for h in range(PEER_HEADS):
                term = jnp.where(r2_ref[h, c] < row_tile(n1_ref, h, r, c), e2_ref[h, c], zero)
                term = term * row_tile(c1_ref, h, r, c)
                g = term if g is None else g + term
            out.append(g)
        return jnp.concatenate(out, axis=0)

    upc = rows // UNIT_ROWS

    def accumulate(c, w):
        cs = slice(c * SLAB, (c + 1) * SLAB)
        acc_ref[:, cs] += jnp.dot(vt_ref[...], w, preferred_element_type=f32)

    a, g, w = {}, {}, {}
    for t in range(len(units) + 3):
        if t < len(units):
            a[t] = activations(units[t])
        if 0 <= t - 1 < len(units):
            g[t - 1] = gates(units[t - 1])
        if 0 <= t - 2 < len(units):
            w[t - 2] = _gelu(a.pop(t - 2).astype(bf16)) * g.pop(t - 2)
        if 0 <= t - 3 < len(units) and (t - 3) % upc == upc - 1:
            c = (t - 3) // upc
            accumulate(c, jnp.concatenate([w.pop(c * upc + k) for k in range(upc)], axis=0))

    @pl.when(e == pl.num_programs(1) - 1)
    def _():
        y = x_ref[...] + acc_ref[...].T
        o_ref[...] = _rms(y, fg_ref[...]) if final else y


def _peer_main(h2t, u, vt, r2, e2, n1, c1, x2, fg, layer, tb, rows, final):
    T = h2t.shape[1]
    eb = rows * N_KEYS
    sel = pl.BlockSpec((PEER_HEADS, tb // SLAB, N_KEYS, SLAB), lambda t, e: (0, t, 0, 0))
    rsel = pl.BlockSpec((PEER_HEADS, tb // LANES, N_KEYS, LANES), lambda t, e: (0, t, 0, 0))
    return pl.pallas_call(
        functools.partial(_peermain_kernel, rows=rows, final=final),
        grid=(T // tb, N_KEYS // rows),
        in_specs=[pl.BlockSpec((D_MODEL, tb), lambda t, e: (0, t)),
                  pl.BlockSpec((None, eb, D_MODEL), lambda t, e: (layer, e, 0)),
                  pl.BlockSpec((None, D_MODEL, eb), lambda t, e: (layer, 0, e)),
                  sel, sel, rsel, rsel,
                  pl.BlockSpec((tb, D_MODEL), lambda t, e: (t, 0)),
                  pl.BlockSpec((1, D_MODEL), lambda t, e: (0, 0))],
        out_specs=pl.BlockSpec((tb, D_MODEL), lambda t, e: (t, 0)),
        out_shape=jax.ShapeDtypeStruct((T, D_MODEL), f32),
        scratch_shapes=[pltpu.VMEM((D_MODEL, tb), f32)],
        compiler_params=_cparams(("parallel", "arbitrary")),
        name="peer_main",
    )(h2t, u, vt, r2, e2, n1, c1, x2, fg)


def _block_sizes(S):
    return dict(tb_in=min(1024, S), tc=min(512, S), tq=min(512, S), heads_per_step=8,
                tb_mix=min(512, S), tb_prep=min(2048, S), tb_peer=min(512, S), rows=16)


def kernel(x, norm1_g, w_in, b_forget, sgu_norm_g, w_spatial, b_spatial, conv_w,
           mix_norm_g, w_out, norm2_g, w_query, sub_keys, expert_u, expert_v, final_g):
    B, S, D = x.shape
    T = B * S
    depth = w_in.shape[0]
    bs = _block_sizes(S)
    x2 = x.reshape(T, D)
    u_all = expert_u.astype(bf16)
    vt_all = expert_v.transpose(0, 2, 1).astype(bf16)
    for l in range(depth):
        w_main, wf = _in_proj_weights(w_in[l])
        bfp = jnp.pad(b_forget[l], (0, LANES - ATT_HEADS)).reshape(1, LANES)
        qa, kp, vt, gm, cv, logf, nrm = _in_proj(x2, norm1_g[l].reshape(1, D), w_main, wf, bfp,
                                                 B, S, bs["tb_in"])
        ka = _cumsum(logf, kp, bs["tc"])
        jstart = _first_key_block(nrm, ka, bs["tq"], bs["heads_per_step"])
        oatt_t = _attention(jstart, qa, ka, vt, bs["tq"], bs["heads_per_step"])
        bspf = jnp.repeat(b_spatial[l].T, GROUP_DIM, axis=1)
        cw = jnp.pad(conv_w[l], ((0, 8 - CONV_K), (0, 0)))
        x2, h2t = _mix_out(oatt_t, gm, cv, x2, sgu_norm_g[l].reshape(1, GMLP_WIDTH),
                          w_spatial[l], bspf, cw, mix_norm_g[l].reshape(1, MIX_WIDTH),
                          w_out[l].astype(bf16), norm2_g[l].reshape(1, D), S, bs["tb_mix"])
        r2, e2, n1, c1 = _peer_prep(h2t, w_query[l].T.astype(bf16), sub_keys[l].astype(bf16),
                                    bs["tb_prep"])
        x2 = _peer_main(h2t, u_all, vt_all, r2, e2, n1, c1, x2, final_g.reshape(1, D), l,
                        bs["tb_peer"], bs["rows"], final=(l == depth - 1))
    return x2.reshape(B, S, D)
```

```python
import functools
import math

import jax
import jax.numpy as jnp
from jax import lax
from jax.experimental import pallas as pl
from jax.experimental.pallas import tpu as pltpu

f32 = jnp.float32
bf16 = jnp.bfloat16

D_MODEL = 1024
ATT_WIDTH = 512
ATT_HEADS = 8
HEAD_DIM = 64
GMLP_WIDTH = 256
GMLP_GROUPS = 4
GROUP_DIM = 64
CHUNK = 128
CONV_WIDTH = 256
CONV_K = 3
MIX_WIDTH = 1024
PEER_HEADS = 8
N_KEYS = 128
PEER_TOPK = 16
D_KEY = 256
D_HALF = 128
EPS = 1e-6
LANES = 128
BF16_ROWS = 16
SLAB = 256
UNIT_ROWS = 8
NEG_BIG = -1e30
LOG2E = math.log2(math.e)
GELU_K0 = -2.0 * math.sqrt(2.0 / math.pi)
GELU_K1 = GELU_K0 * 0.044715
F_PARTS = 3
V_ROWS = 80
UNDERFLOW_LOG2 = 160.0
NORM_SLACK = 1.02
RANK_MARK = -3.0e38
RANK_STEP = 2.0 ** -10
CAND_PACKING = (((2, 5), (4, 3)), ((3, 4), (5, 2), (6, 2)), ((7, 2),))

VMEM_LIMIT = 56 * 1024 * 1024

_NT = (((1,), (1,)), ((), ()))


def _cparams(sem, flags=None):
    return pltpu.CompilerParams(dimension_semantics=sem, vmem_limit_bytes=VMEM_LIMIT, flags=flags)


def _gelu(x):
    return x / (1.0 + jnp.exp(x * (GELU_K0 + GELU_K1 * (x * x))))


def _rms(x, g):
    return x * lax.rsqrt(jnp.mean(x * x, axis=-1, keepdims=True) + EPS) * g


def _split3(x):
    hi = x.astype(bf16)
    r = x - hi.astype(f32)
    mid = r.astype(bf16)
    lo = (r - mid.astype(f32)).astype(bf16)
    return hi, mid, lo


def _inproj_kernel(x_ref, g_ref, w_ref, wf_ref, bf_ref,
                   qa_ref, kp_ref, vt_ref, gm_ref, cv_ref, logf_ref, nrm_ref):
    tb = x_ref.shape[0]
    h = _rms(x_ref[...], g_ref[...])
    hb = h.astype(bf16)
    z = jnp.dot(hb, w_ref[...], preferred_element_type=f32)
    hw = ATT_HEADS * LANES
    lane = lax.broadcasted_iota(jnp.int32, (1, hw), 1) % LANES
    fcol = (lane >= HEAD_DIM) & (lane < HEAD_DIM + F_PARTS)
    zq = z[:, :hw] * (HEAD_DIM ** -0.5 * LOG2E)
    zk = z[:, hw:2 * hw]
    qa = jnp.where(fcol, -1.0, zq).astype(bf16)
    kp = zk.astype(bf16)
    vT = z[:, 2 * hw:2 * hw + ATT_WIDTH].T
    tail = jnp.where(lax.broadcasted_iota(jnp.int32, (V_ROWS - HEAD_DIM, tb), 0) == 0,
                     1.0, 0.0).astype(bf16)
    for hd in range(ATT_HEADS):
        qa_ref[0, hd] = qa[:, hd * LANES:(hd + 1) * LANES]
        kp_ref[0, hd] = kp[:, hd * LANES:(hd + 1) * LANES]
        vt_ref[0, hd, :HEAD_DIM, :] = vT[hd * HEAD_DIM:(hd + 1) * HEAD_DIM, :].astype(bf16)
        vt_ref[0, hd, HEAD_DIM:, :] = tail
    c0 = 2 * hw + ATT_WIDTH
    gm_ref[...] = z[:, c0:c0 + 2 * GMLP_WIDTH]
    cv_ref[...] = z[:, c0 + 2 * GMLP_WIDTH:]
    r = lax.broadcasted_iota(jnp.int32, (hw, LANES), 0) // LANES
    c = lax.broadcasted_iota(jnp.int32, (hw, LANES), 1)
    ind_q = jnp.where(r == c, 1.0, 0.0).astype(bf16)
    ind_k = jnp.where(r + ATT_HEADS == c, 1.0, 0.0).astype(bf16)
    nrm_ref[...] = (jnp.dot((zq * zq).astype(bf16), ind_q, preferred_element_type=f32)
                    + jnp.dot((zk * zk).astype(bf16), ind_k, preferred_element_type=f32))
    hl = (h - hb.astype(f32)).astype(bf16)
    wf = wf_ref[...]
    wfh = wf.astype(bf16)
    wfl = (wf - wfh.astype(f32)).astype(bf16)
    zf = (jnp.dot(hb, wfh, preferred_element_type=f32)
          + jnp.dot(hb, wfl, preferred_element_type=f32)
          + jnp.dot(hl, wfh, preferred_element_type=f32)) + bf_ref[...]
    logf_ref[...] = jnp.minimum(zf, 0.0) - jnp.log1p(jnp.exp(-jnp.abs(zf)))


def _in_proj(x2, g, w, wf, bfp, B, S, tb):
    T = x2.shape[0]
    bps = S // tb
    ncol = w.shape[1]
    const = lambda i: (0, 0)
    row = lambda i: (i, 0)
    head_spec = pl.BlockSpec((1, ATT_HEADS, tb, LANES), lambda i: (i // bps, 0, i % bps, 0))
    head_shape = jax.ShapeDtypeStruct((B, ATT_HEADS, S, LANES), bf16)
    return pl.pallas_call(
        _inproj_kernel,
        grid=(T // tb,),
        in_specs=[pl.BlockSpec((tb, D_MODEL), row),
                  pl.BlockSpec((1, D_MODEL), const),
                  pl.BlockSpec((D_MODEL, ncol), const),
                  pl.BlockSpec((D_MODEL, LANES), const),
                  pl.BlockSpec((1, LANES), const)],
        out_specs=[head_spec, head_spec,
                   pl.BlockSpec((1, ATT_HEADS, V_ROWS, tb), lambda i: (i // bps, 0, 0, i % bps)),
                   pl.BlockSpec((tb, 2 * GMLP_WIDTH), row),
                   pl.BlockSpec((tb, 3 * CONV_WIDTH), row),
                   pl.BlockSpec((tb, LANES), row),
                   pl.BlockSpec((tb, LANES), row)],
        out_shape=[head_shape, head_shape,
                   jax.ShapeDtypeStruct((B, ATT_HEADS, V_ROWS, S), bf16),
                   jax.ShapeDtypeStruct((T, 2 * GMLP_WIDTH), f32),
                   jax.ShapeDtypeStruct((T, 3 * CONV_WIDTH), f32),
                   jax.ShapeDtypeStruct((T, LANES), f32),
                   jax.ShapeDtypeStruct((T, LANES), f32)],
        compiler_params=_cparams(("parallel",)),
        name="in_proj",
    )(x2, g, w, wf, bfp)


def _in_proj_weights(wl):
    f0 = 3 * ATT_WIDTH

    def per_head(cols):
        w3 = cols.reshape(D_MODEL, ATT_HEADS, HEAD_DIM)
        return jnp.pad(w3, ((0, 0), (0, 0), (0, LANES - HEAD_DIM))).reshape(D_MODEL, -1)

    w_main = jnp.concatenate([per_head(wl[:, :ATT_WIDTH]), per_head(wl[:, ATT_WIDTH:2 * ATT_WIDTH]),
                              wl[:, 2 * ATT_WIDTH:f0], wl[:, f0 + ATT_HEADS:]], axis=1)
    wf = jnp.pad(wl[:, f0:f0 + ATT_HEADS], ((0, 0), (0, LANES - ATT_HEADS)))
    return w_main.astype(bf16), wf


def _cumsum_kernel(x_ref, kp_ref, ka_ref, carry_ref, *, tc):
    @pl.when(pl.program_id(1) == 0)
    def _():
        carry_ref[...] = jnp.zeros_like(carry_ref)

    row = lax.broadcasted_iota(jnp.int32, (tc, tc), 0)
    col = lax.broadcasted_iota(jnp.int32, (tc, tc), 1)
    tri = jnp.where(col <= row, 1.0, 0.0).astype(bf16)
    hi, mid, lo = _split3(x_ref[...])
    c = (jnp.dot(tri, hi, preferred_element_type=f32)
         + jnp.dot(tri, mid, preferred_element_type=f32)
         + jnp.dot(tri, lo, preferred_element_type=f32)) + carry_ref[...]
    carry_ref[...] = c[tc - 1:tc, :]
    hw = ATT_HEADS * LANES
    src = lax.broadcasted_iota(jnp.int32, (LANES, hw), 0)
    dst = lax.broadcasted_iota(jnp.int32, (LANES, hw), 1)
    fcols = jnp.zeros((tc, hw), f32)
    for p, part in enumerate(_split3(c * LOG2E)):
        place = jnp.where(dst == src * LANES + HEAD_DIM + p, 1.0, 0.0).astype(bf16)
        fcols = fcols + jnp.dot(part, place, preferred_element_type=f32)
    for hd in range(ATT_HEADS):
        ka_ref[0, hd] = (kp_ref[0, hd].astype(f32)
                         + fcols[:, hd * LANES:(hd + 1) * LANES]).astype(bf16)


def _cumsum(logf, kp, tc):
    B, H, S, _ = kp.shape
    nb = S // tc
    head_spec = pl.BlockSpec((1, H, tc, LANES), lambda b, j: (b, 0, j, 0))
    return pl.pallas_call(
        functools.partial(_cumsum_kernel, tc=tc),
        grid=(B, nb),
        in_specs=[pl.BlockSpec((tc, LANES), lambda b, j: (b * nb + j, 0)), head_spec],
        out_specs=head_spec,
        out_shape=jax.ShapeDtypeStruct(kp.shape, bf16),
        scratch_shapes=[pltpu.VMEM((1, LANES), f32)],
        compiler_params=_cparams(("arbitrary", "arbitrary")),
        name="cumsum_logf",
    )(logf, kp)


def _attn_kernel(jstart_ref, qa_ref, ka_ref, vt_ref, o_ref, acc_ref, m_ref, *, tq, hg):
    i = pl.program_id(2)
    m_ref[...] = jnp.full(m_ref.shape, NEG_BIG, f32)
    acc_ref[...] = jnp.zeros(acc_ref.shape, f32)

    def step(j, masked):
        start = pl.multiple_of(j * tq, tq)

        def scores(h):
            k = ka_ref[0, h, pl.ds(start, tq), :]
            s = lax.dot_general(k, qa_ref[0, h], _NT, preferred_element_type=f32)
            if masked:
                r = lax.broadcasted_iota(jnp.int32, (tq, tq), 0)
                c = lax.broadcasted_iota(jnp.int32, (tq, tq), 1)
                s = jnp.where(r <= c, s, NEG_BIG)
            return s

        def new_max(h, s):
            return jnp.maximum(m_ref[h], jnp.max(s, axis=0, keepdims=True))

        def weights(s, m_new):
            return jnp.exp2(s - m_new).astype(bf16)

        def accumulate(h, p, m_new):
            pv = jnp.dot(vt_ref[0, h, :, pl.ds(start, tq)], p, preferred_element_type=f32)
            acc_ref[h] = jnp.exp2(m_ref[h] - m_new) * acc_ref[h] + pv
            m_ref[h] = m_new

        s, mx, p = {}, {}, {}
        for t in range(hg + 3):
            if t < hg:
                s[t] = scores(t)
            if 0 <= t - 1 < hg:
                mx[t - 1] = new_max(t - 1, s[t - 1])
            if 0 <= t - 2 < hg:
                p[t - 2] = weights(s.pop(t - 2), mx[t - 2])
            if 0 <= t - 3 < hg:
                accumulate(t - 3, p.pop(t - 3), mx.pop(t - 3))

    def body(j, carry):
        step(j, False)
        return carry

    lax.fori_loop(jstart_ref[pl.program_id(0), pl.program_id(1), i], i, body, 0)
    step(i, True)
    for h in range(hg):
        o_ref[0, h * HEAD_DIM:(h + 1) * HEAD_DIM, :] = (
            acc_ref[h, :HEAD_DIM, :] / acc_ref[h, HEAD_DIM:HEAD_DIM + 1, :])


def _attention(jstart, qa, ka, vt, tq, hg):
    B, H, S, _ = qa.shape
    return pl.pallas_call(
        functools.partial(_attn_kernel, tq=tq, hg=hg),
        grid_spec=pltpu.PrefetchScalarGridSpec(
            num_scalar_prefetch=1,
            grid=(B, H // hg, S // tq),
            in_specs=[pl.BlockSpec((1, hg, tq, LANES), lambda b, g, i, js: (b, g, i, 0)),
                      pl.BlockSpec((1, hg, S, LANES), lambda b, g, i, js: (b, g, 0, 0),
                                   pipeline_mode=pl.Buffered(1)),
                      pl.BlockSpec((1, hg, V_ROWS, S), lambda b, g, i, js: (b, g, 0, 0),
                                   pipeline_mode=pl.Buffered(1))],
            out_specs=pl.BlockSpec((1, hg * HEAD_DIM, tq), lambda b, g, i, js: (b, g, i)),
            scratch_shapes=[pltpu.VMEM((hg, V_ROWS, tq), f32), pltpu.VMEM((hg, 1, tq), f32)]),
        out_shape=jax.ShapeDtypeStruct((B, H * HEAD_DIM, S), f32),
        compiler_params=_cparams(("parallel", "parallel", "arbitrary")),
        name="fox_attention",
    )(jstart, qa, ka, vt)


def _first_key_block(nrm, ka, tq, hg):
    B, H, S, _ = ka.shape
    nq = S // tq
    n2 = jnp.max(nrm.reshape(B, nq, tq, LANES), axis=2)
    qmax = jnp.sqrt(n2[:, :, :H]).transpose(0, 2, 1) * NORM_SLACK
    kmax = jnp.sqrt(n2[:, :, H:2 * H]).transpose(0, 2, 1) * NORM_SLACK
    blk = jnp.arange(nq, dtype=jnp.int32)
    earlier = blk[None, :] <= blk[:, None]
    kcum = jnp.max(jnp.where(earlier[None, None], kmax[:, :, None, :], 0.0), axis=3)
    fl = ka[:, :, :, HEAD_DIM:HEAD_DIM + F_PARTS].astype(f32).sum(-1)
    f_end = fl[:, :, tq - 1::tq]
    f_start = fl[:, :, ::tq]
    upper = qmax[:, :, :, None] * kcum[:, :, None, :] - f_end[:, :, None, :]
    lower = -qmax * kmax - f_start
    skip = upper < lower[:, :, :, None] - UNDERFLOW_LOG2
    lead = jnp.min(jnp.where(skip, nq, blk[None, None, None, :]), axis=3)
    lead = jnp.minimum(lead, blk[None, None, :])
    return lead.reshape(B, H // hg, hg, nq).min(axis=2).astype(jnp.int32)


def _mixout_kernel(oatt_ref, gm_ref, cv_ref, cvp_ref, x_ref, sgug_ref, wsp_ref, bsp_ref,
                   cw_ref, mg_ref, wout_ref, n2g_ref, xo_ref, h2_ref, *, tb, blocks_per_seq):
    i = pl.program_id(0)
    gm = gm_ref[...]
    u = _gelu(gm[:, :GMLP_WIDTH])
    v = _gelu(gm[:, GMLP_WIDTH:])
    gr = lax.broadcasted_iota(jnp.int32, (GMLP_WIDTH, GMLP_WIDTH), 0) // GROUP_DIM
    gc = lax.broadcasted_iota(jnp.int32, (GMLP_WIDTH, GMLP_WIDTH), 1) // GROUP_DIM
    grp = jnp.where(gr == gc, 1.0 / GROUP_DIM, 0.0).astype(bf16)
    vh, vm, vl = _split3(v * v)
    ms = (jnp.dot(vh, grp, preferred_element_type=f32)
          + jnp.dot(vm, grp, preferred_element_type=f32)
          + jnp.dot(vl, grp, preferred_element_type=f32))
    vn = (v * lax.rsqrt(ms + EPS) * sgug_ref[...]).astype(bf16)
    tr = lax.broadcasted_iota(jnp.int32, (CHUNK, CHUNK), 0)
    tc = lax.broadcasted_iota(jnp.int32, (CHUNK, CHUNK), 1)
    lane_g = lax.broadcasted_iota(jnp.int32, (CHUNK, GMLP_WIDTH), 1) // GROUP_DIM
    mixed_chunks = []
    for c in range(tb // CHUNK):
        vc = vn[c * CHUNK:(c + 1) * CHUNK, :]
        mixed = jnp.zeros((CHUNK, GMLP_WIDTH), f32)
        for g in range(GMLP_GROUPS):
            wg = jnp.where(tc <= tr, wsp_ref[g], 0.0).astype(bf16)
            mg = jnp.dot(wg, vc, preferred_element_type=f32)
            mixed = jnp.where(lane_g == g, mg, mixed)
        mixed_chunks.append(mixed + bsp_ref[...])
    o_gmlp = u * jnp.concatenate(mixed_chunks, axis=0)
    cv = cv_ref[...]
    c_in = cv[:, :CONV_WIDTH]
    c_b = cv[:, CONV_WIDTH:2 * CONV_WIDTH]
    c_c = cv[:, 2 * CONV_WIDTH:]
    z = c_c * c_in
    cvp = cvp_ref[...]
    zp = cvp[:, 2 * CONV_WIDTH:] * cvp[:, :CONV_WIDTH]
    zp = jnp.where(i % blocks_per_seq == 0, jnp.zeros_like(zp), zp)
    rowi = lax.broadcasted_iota(jnp.int32, (tb, CONV_WIDTH), 0)
    z1 = pltpu.roll(z, 1, axis=0)
    z1 = jnp.where(rowi == 0, zp[7:8, :], z1)
    z2 = pltpu.roll(z, 2, axis=0)
    z2 = jnp.where(rowi == 0, zp[6:7, :], z2)
    z2 = jnp.where(rowi == 1, zp[7:8, :], z2)
    cw = cw_ref[...]
    o_conv = c_b * (cw[0:1, :] * z2 + cw[1:2, :] * z1 + cw[2:3, :] * z)
    mg_ = mg_ref[...]
    o_att = oatt_ref[0].T
    mixed_all = jnp.concatenate([
        _rms(o_att, mg_[:, :ATT_WIDTH]),
        _rms(o_gmlp, mg_[:, ATT_WIDTH:ATT_WIDTH + GMLP_WIDTH]),
        _rms(o_conv, mg_[:, ATT_WIDTH + GMLP_WIDTH:])], axis=-1).astype(bf16)
    xn = x_ref[...] + jnp.dot(mixed_all, wout_ref[...], preferred_element_type=f32)
    xo_ref[...] = xn
    h2_ref[...] = _rms(xn, n2g_ref[...]).T.astype(bf16)


def _mix_out(oatt_t, gm, cv, x2, sgug, wsp, bspf, cw, mg, wout, n2g, S, tb):
    T = x2.shape[0]
    bps = S // tb
    const2 = lambda i: (0, 0)
    row = lambda i: (i, 0)
    return pl.pallas_call(
        functools.partial(_mixout_kernel, tb=tb, blocks_per_seq=bps),
        grid=(T // tb,),
        in_specs=[pl.BlockSpec((1, ATT_WIDTH, tb), lambda i: (i // bps, 0, i % bps)),
                  pl.BlockSpec((tb, 2 * GMLP_WIDTH), row),
                  pl.BlockSpec((tb, 3 * CONV_WIDTH), row),
                  pl.BlockSpec((8, 3 * CONV_WIDTH),
                               lambda i: (jnp.maximum(i * (tb // 8) - 1, 0), 0)),
                  pl.BlockSpec((tb, D_MODEL), row),
                  pl.BlockSpec((1, GMLP_WIDTH), const2),
                  pl.BlockSpec((GMLP_GROUPS, CHUNK, CHUNK), lambda i: (0, 0, 0)),
                  pl.BlockSpec((CHUNK, GMLP_WIDTH), const2),
                  pl.BlockSpec((8, CONV_WIDTH), const2),
                  pl.BlockSpec((1, MIX_WIDTH), const2),
                  pl.BlockSpec((MIX_WIDTH, D_MODEL), const2),
                  pl.BlockSpec((1, D_MODEL), const2)],
        out_specs=[pl.BlockSpec((tb, D_MODEL), row),
                   pl.BlockSpec((D_MODEL, tb), lambda i: (0, i))],
        out_shape=[jax.ShapeDtypeStruct((T, D_MODEL), f32),
                   jax.ShapeDtypeStruct((D_MODEL, T), bf16)],
        compiler_params=_cparams(("parallel",)),
        name="mix_out",
    )(oatt_t, gm, cv, cv, x2, sgug, wsp, bspf, cw, mg, wout, n2g)


def _colmax(x):
    return jnp.max(x, axis=0, keepdims=True)


def _colsum(x):
    return jnp.sum(x, axis=0, keepdims=True)


def _tree(op, xs):
    while len(xs) > 1:
        xs = [op(xs[i], xs[i + 1]) if i + 1 < len(xs) else xs[i] for i in range(0, len(xs), 2)]
    return xs[0]


def _top16(scores, vals_refs):
    scores = list(scores)
    for a in range(PEER_TOPK):
        for i, s in enumerate(scores):
            m = _colmax(s)
            scores[i] = jnp.where(s == m, RANK_MARK * (1.0 + a * RANK_STEP), s)
            vals_refs[i][a:a + 1, :] = m
    ranks = []
    for s in scores:
        rank = jnp.round((s - RANK_MARK) * (1.0 / (RANK_MARK * RANK_STEP)))
        ranks.append(jnp.where(s <= RANK_MARK, rank, float(PEER_TOPK)))
    return ranks


def _select_tables(s1, s2, v1_ref, v2_ref):
    r1, r2 = _top16([s1, s2], [v1_ref, v2_ref])
    v1 = v1_ref[...]
    v2 = v2_ref[...]
    row8 = lax.broadcasted_iota(jnp.int32, (8, s1.shape[1]), 0)
    v2lo = v2[0:8]

    def packed(segments):
        arr, off = None, 0
        for a, nb in segments:
            src = v1[a:a + 1] + (v2lo if off == 0 else pltpu.roll(v2lo, off, axis=0))
            arr = src if arr is None else jnp.where(row8 >= off, src, arr)
            off += nb
        return arr if off == 8 else jnp.where(row8 >= off, -jnp.inf, arr)

    cands = [v1[0:1] + v2lo, v1[0:1] + v2[8:16], v1[1:2] + v2lo, v1[8:16] + v2[0:1]]
    cands += [packed(seg) for seg in CAND_PACKING]
    cmax = v1[0:1] + v2[0:1]
    work = list(cands)
    tau = cmax
    cnt = jnp.zeros_like(cmax)
    for _ in range(PEER_TOPK):
        m = _colmax(_tree(jnp.maximum, work))
        tau = jnp.where(cnt < float(PEER_TOPK), m, tau)
        hits = [w == m for w in work]
        cnt = cnt + _colsum(_tree(jnp.add, [jnp.where(h, 1.0, 0.0) for h in hits]))
        work = [jnp.where(h, -jnp.inf, w) for h, w in zip(hits, work)]
    sel = [c >= tau for c in cands]
    zsum = _colsum(_tree(jnp.add, [jnp.where(s_, jnp.exp(c - cmax), 0.0)
                                   for s_, c in zip(sel, cands)]))
    ones = [jnp.where(s_, 1.0, 0.0) for s_ in sel]
    n_lo = {0: _colsum(ones[0] + ones[1]), 1: _colsum(ones[2])}
    for o, segments in zip(ones[4:], CAND_PACKING):
        off = 0
        for a, nb in segments:
            inside = jnp.where(row8 >= off, jnp.where(row8 < off + nb, o, 0.0), 0.0)
            n_lo[a] = _colsum(inside)
            off += nb
    n_hi = ones[3]
    n1 = jnp.zeros(r1.shape, f32)
    for a in range(8):
        n1 = jnp.where(r1 == float(a), n_lo[a], n1)
    for a in range(8, PEER_TOPK):
        n1 = jnp.where(r1 == float(a), n_hi[a - 8:a - 7], n1)
    e2 = jnp.exp(s2 - v2[0:1])
    c1 = jnp.exp(s1 - v1[0:1]) / zsum
    return r2.astype(bf16), e2.astype(bf16), n1, c1


def _peerprep_kernel(h2_ref, wq_ref, keys_ref, r2_ref, e2_ref, n1_ref, c1_ref,
                     v1_ref, v2_ref):
    qT = jnp.dot(wq_ref[...], h2_ref[...], preferred_element_type=f32)
    qb = qT.astype(bf16)
    s1 = jnp.dot(keys_ref[0], qb[:D_HALF], preferred_element_type=f32)
    s2 = jnp.dot(keys_ref[1], qb[D_HALF:], preferred_element_type=f32)
    lpc = SLAB // LANES
    for c in range(r2_ref.shape[1]):
        r2s, e2s = [], []
        for k in range(lpc):
            t = c * lpc + k
            ls = slice(t * LANES, (t + 1) * LANES)
            r2, e2, n1, c1 = _select_tables(s1[:, ls], s2[:, ls], v1_ref, v2_ref)
            r2s.append(r2)
            e2s.append(e2)
            n1_ref[0, t] = n1
            c1_ref[0, t] = c1
        r2_ref[0, c] = jnp.concatenate(r2s, axis=1)
        e2_ref[0, c] = jnp.concatenate(e2s, axis=1)


def _peer_prep(h2t, wqT, keys, tb):
    T = h2t.shape[1]
    spec = pl.BlockSpec((1, tb // SLAB, N_KEYS, SLAB), lambda t, h: (h, t, 0, 0))
    shp = (PEER_HEADS, T // SLAB, N_KEYS, SLAB)
    rspec = pl.BlockSpec((1, tb // LANES, N_KEYS, LANES), lambda t, h: (h, t, 0, 0))
    rshp = (PEER_HEADS, T // LANES, N_KEYS, LANES)
    return pl.pallas_call(
        _peerprep_kernel,
        grid=(T // tb, PEER_HEADS),
        in_specs=[pl.BlockSpec((D_MODEL, tb), lambda t, h: (0, t)),
                  pl.BlockSpec((D_KEY, D_MODEL), lambda t, h: (h, 0)),
                  pl.BlockSpec((2, N_KEYS, D_HALF), lambda t, h: (0, 0, 0))],
        out_specs=[spec, spec, rspec, rspec],
        out_shape=[jax.ShapeDtypeStruct(shp, bf16), jax.ShapeDtypeStruct(shp, bf16),
                   jax.ShapeDtypeStruct(rshp, f32), jax.ShapeDtypeStruct(rshp, f32)],
        scratch_shapes=[pltpu.VMEM((PEER_TOPK, LANES), f32), pltpu.VMEM((PEER_TOPK, LANES), f32)],
        compiler_params=_cparams(("parallel", "arbitrary")),
        name="peer_prep",
    )(h2t, wqT, keys)


def _peermain_kernel(h2_ref, u_ref, v_ref, r2_ref, e2_ref, n1_ref, c1_ref, x_ref, fg_ref,
                     o_ref, acc_ref, *, rows, final):
    e = pl.program_id(1)
    tb = h2_ref.shape[1]
    nc = tb // SLAB
    lpc = SLAB // LANES

    @pl.when(e == 0)
    def _():
        acc_ref[...] = jnp.zeros_like(acc_ref)

    base = e * rows

    def row_tile(ref, h, r, c):
        row = jnp.concatenate(
            [jnp.broadcast_to(ref[h, c * lpc + k, pl.ds(base + r, 1), :], (BF16_ROWS, LANES))
             for k in range(lpc)], axis=1).astype(bf16)
        return jnp.tile(row, (N_KEYS // BF16_ROWS, 1))

    zero = jnp.zeros((N_KEYS, SLAB), bf16)

    ub = UNIT_ROWS * N_KEYS
    units = [(c, u) for c in range(nc) for u in range(rows // UNIT_ROWS)]

    def activations(unit):
        c, u = unit
        return jnp.dot(u_ref[u * ub:(u + 1) * ub, :], h2_ref[:, c * SLAB:(c + 1) * SLAB],
                       preferred_element_type=f32)

    def gates(unit):
        c, u = unit
        out = []
        for r in range(u * UNIT_ROWS, (u + 1) * UNIT_ROWS):
            g = None
            for h in range(PEER_HEADS):
                term = jnp.where(r2_ref[h, c] < row_tile(n1_ref, h, r, c), e2_ref[h, c], zero)
                term = term * row_tile(c1_ref, h, r, c)
                g = term if g is None else g + term
            out.append(g)
        return jnp.concatenate(out, axis=0)

    upc = rows // UNIT_ROWS

    def accumulate(c, w):
        cs = slice(c * SLAB, (c + 1) * SLAB)
        acc_ref[:, cs] += lax.dot_general(v_ref[...], w, (((0,), (0,)), ((), ())),
                                          preferred_element_type=f32)

    a, g, w = {}, {}, {}
    for t in range(len(units) + 3):
        if t < len(units):
            a[t] = activations(units[t])
        if 0 <= t - 1 < len(units):
            g[t - 1] = gates(units[t - 1])
        if 0 <= t - 2 < len(units):
            w[t - 2] = _gelu(a.pop(t - 2).astype(bf16)) * g.pop(t - 2)
        if 0 <= t - 3 < len(units) and (t - 3) % upc == upc - 1:
            c = (t - 3) // upc
            accumulate(c, jnp.concatenate([w.pop(c * upc + k) for k in range(upc)], axis=0))

    @pl.when(e == pl.num_programs(1) - 1)
    def _():
        y = x_ref[...] + acc_ref[...].T
        o_ref[...] = _rms(y, fg_ref[...]) if final else y


def _peer_main(h2t, u, v, r2, e2, n1, c1, x2, fg, layer, tb, rows, final):
    T = h2t.shape[1]
    eb = rows * N_KEYS
    sel = pl.BlockSpec((PEER_HEADS, tb // SLAB, N_KEYS, SLAB), lambda t, e: (0, t, 0, 0))
    rsel = pl.BlockSpec((PEER_HEADS, tb // LANES, N_KEYS, LANES), lambda t, e: (0, t, 0, 0))
    return pl.pallas_call(
        functools.partial(_peermain_kernel, rows=rows, final=final),
        grid=(T // tb, N_KEYS // rows),
        in_specs=[pl.BlockSpec((D_MODEL, tb), lambda t, e: (0, t)),
                  pl.BlockSpec((None, eb, D_MODEL), lambda t, e: (layer, e, 0)),
                  pl.BlockSpec((None, eb, D_MODEL), lambda t, e: (layer, e, 0)),
                  sel, sel, rsel, rsel,
                  pl.BlockSpec((tb, D_MODEL), lambda t, e: (t, 0)),
                  pl.BlockSpec((1, D_MODEL), lambda t, e: (0, 0))],
        out_specs=pl.BlockSpec((tb, D_MODEL), lambda t, e: (t, 0)),
        out_shape=jax.ShapeDtypeStruct((T, D_MODEL), f32),
        scratch_shapes=[pltpu.VMEM((D_MODEL, tb), f32)],
        compiler_params=_cparams(("parallel", "arbitrary")),
        name="peer_main",
    )(h2t, u, v, r2, e2, n1, c1, x2, fg)


def _block_sizes(S):
    return dict(tb_in=min(1024, S), tc=min(512, S), tq=min(512, S), heads_per_step=8,
                tb_mix=min(512, S), tb_prep=min(2048, S), tb_peer=min(512, S), rows=16)


def kernel(x, norm1_g, w_in, b_forget, sgu_norm_g, w_spatial, b_spatial, conv_w,
           mix_norm_g, w_out, norm2_g, w_query, sub_keys, expert_u, expert_v, final_g):
    B, S, D = x.shape
    T = B * S
    depth = w_in.shape[0]
    bs = _block_sizes(S)
    x2 = x.reshape(T, D)
    u_all = expert_u.astype(bf16)
    v_all = expert_v.astype(bf16)
    for l in range(depth):
        w_main, wf = _in_proj_weights(w_in[l])
        bfp = jnp.pad(b_forget[l], (0, LANES - ATT_HEADS)).reshape(1, LANES)
        qa, kp, vt, gm, cv, logf, nrm = _in_proj(x2, norm1_g[l].reshape(1, D), w_main, wf, bfp,
                                                 B, S, bs["tb_in"])
        ka = _cumsum(logf, kp, bs["tc"])
        jstart = _first_key_block(nrm, ka, bs["tq"], bs["heads_per_step"])
        oatt_t = _attention(jstart, qa, ka, vt, bs["tq"], bs["heads_per_step"])
        bspf = jnp.repeat(b_spatial[l].T, GROUP_DIM, axis=1)
        cw = jnp.pad(conv_w[l], ((0, 8 - CONV_K), (0, 0)))
        x2, h2t = _mix_out(oatt_t, gm, cv, x2, sgu_norm_g[l].reshape(1, GMLP_WIDTH),
                          w_spatial[l], bspf, cw, mix_norm_g[l].reshape(1, MIX_WIDTH),
                          w_out[l].astype(bf16), norm2_g[l].reshape(1, D), S, bs["tb_mix"])
        r2, e2, n1, c1 = _peer_prep(h2t, w_query[l].T.astype(bf16), sub_keys[l].astype(bf16),
                                    bs["tb_prep"])
        x2 = _peer_main(h2t, u_all, v_all, r2, e2, n1, c1, x2, final_g.reshape(1, D), l,
                        bs["tb_peer"], bs["rows"], final=(l == depth - 1))
    return x2.reshape(B, S, D)
```

```python
import functools
import math

import jax
import jax.numpy as jnp
from jax import lax
from jax.experimental import pallas as pl
from jax.experimental.pallas import tpu as pltpu

f32 = jnp.float32
bf16 = jnp.bfloat16

D_MODEL = 1024
ATT_WIDTH = 512
ATT_HEADS = 8
HEAD_DIM = 64
GMLP_WIDTH = 256
GMLP_GROUPS = 4
GROUP_DIM = 64
CHUNK = 128
CONV_WIDTH = 256
CONV_K = 3
MIX_WIDTH = 1024
PEER_HEADS = 8
N_KEYS = 128
PEER_TOPK = 16
D_KEY = 256
D_HALF = 128
EPS = 1e-6
LANES = 128
BF16_ROWS = 16
SLAB = 256
UNIT_ROWS = 8
GATE_ROWS = 128
NEG_BIG = -1e30
LOG2E = math.log2(math.e)
GELU_K0 = -2.0 * math.sqrt(2.0 / math.pi)
GELU_K1 = GELU_K0 * 0.044715
F_PARTS = 3
V_ROWS = 80
UNDERFLOW_LOG2 = 160.0
NORM_SLACK = 1.02
RANK_MARK = -3.0e38
RANK_STEP = 2.0 ** -10
CAND_PACKING = (((2, 5), (4, 3)), ((3, 4), (5, 2), (6, 2)), ((7, 2),))

VMEM_LIMIT = 56 * 1024 * 1024

_NT = (((1,), (1,)), ((), ()))


def _cparams(sem, flags=None):
    return pltpu.CompilerParams(dimension_semantics=sem, vmem_limit_bytes=VMEM_LIMIT, flags=flags)


def _gelu(x):
    return x / (1.0 + jnp.exp(x * (GELU_K0 + GELU_K1 * (x * x))))


def _rms(x, g):
    return x * lax.rsqrt(jnp.mean(x * x, axis=-1, keepdims=True) + EPS) * g


def _split3(x):
    hi = x.astype(bf16)
    r = x - hi.astype(f32)
    mid = r.astype(bf16)
    lo = (r - mid.astype(f32)).astype(bf16)
    return hi, mid, lo


def _inproj_kernel(x_ref, g_ref, w_ref, wf_ref, bf_ref,
                   qa_ref, kp_ref, vt_ref, gm_ref, cv_ref, logf_ref, nrm_ref):
    tb = x_ref.shape[0]
    h = _rms(x_ref[...], g_ref[...])
    hb = h.astype(bf16)
    z = jnp.dot(hb, w_ref[...], preferred_element_type=f32)
    hw = ATT_HEADS * LANES
    lane = lax.broadcasted_iota(jnp.int32, (1, hw), 1) % LANES
    fcol = (lane >= HEAD_DIM) & (lane < HEAD_DIM + F_PARTS)
    zq = z[:, :hw] * (HEAD_DIM ** -0.5 * LOG2E)
    zk = z[:, hw:2 * hw]
    qa = jnp.where(fcol, -1.0, zq).astype(bf16)
    kp = zk.astype(bf16)
    vT = z[:, 2 * hw:2 * hw + ATT_WIDTH].T
    tail = jnp.where(lax.broadcasted_iota(jnp.int32, (V_ROWS - HEAD_DIM, tb), 0) == 0,
                     1.0, 0.0).astype(bf16)
    for hd in range(ATT_HEADS):
        qa_ref[0, hd] = qa[:, hd * LANES:(hd + 1) * LANES]
        kp_ref[0, hd] = kp[:, hd * LANES:(hd + 1) * LANES]
        vt_ref[0, hd, :HEAD_DIM, :] = vT[hd * HEAD_DIM:(hd + 1) * HEAD_DIM, :].astype(bf16)
        vt_ref[0, hd, HEAD_DIM:, :] = tail
    c0 = 2 * hw + ATT_WIDTH
    gm_ref[...] = z[:, c0:c0 + 2 * GMLP_WIDTH]
    cv_ref[...] = z[:, c0 + 2 * GMLP_WIDTH:]
    r = lax.broadcasted_iota(jnp.int32, (hw, LANES), 0) // LANES
    c = lax.broadcasted_iota(jnp.int32, (hw, LANES), 1)
    ind_q = jnp.where(r == c, 1.0, 0.0).astype(bf16)
    ind_k = jnp.where(r + ATT_HEADS == c, 1.0, 0.0).astype(bf16)
    nrm_ref[...] = (jnp.dot((zq * zq).astype(bf16), ind_q, preferred_element_type=f32)
                    + jnp.dot((zk * zk).astype(bf16), ind_k, preferred_element_type=f32))
    hl = (h - hb.astype(f32)).astype(bf16)
    wf = wf_ref[...]
    wfh = wf.astype(bf16)
    wfl = (wf - wfh.astype(f32)).astype(bf16)
    zf = (jnp.dot(hb, wfh, preferred_element_type=f32)
          + jnp.dot(hb, wfl, preferred_element_type=f32)
          + jnp.dot(hl, wfh, preferred_element_type=f32)) + bf_ref[...]
    logf_ref[...] = jnp.minimum(zf, 0.0) - jnp.log1p(jnp.exp(-jnp.abs(zf)))


def _in_proj(x2, g, w, wf, bfp, B, S, tb):
    T = x2.shape[0]
    bps = S // tb
    ncol = w.shape[1]
    const = lambda i: (0, 0)
    row = lambda i: (i, 0)
    head_spec = pl.BlockSpec((1, ATT_HEADS, tb, LANES), lambda i: (i // bps, 0, i % bps, 0))
    head_shape = jax.ShapeDtypeStruct((B, ATT_HEADS, S, LANES), bf16)
    return pl.pallas_call(
        _inproj_kernel,
        grid=(T // tb,),
        in_specs=[pl.BlockSpec((tb, D_MODEL), row),
                  pl.BlockSpec((1, D_MODEL), const),
                  pl.BlockSpec((D_MODEL, ncol), const),
                  pl.BlockSpec((D_MODEL, LANES), const),
                  pl.BlockSpec((1, LANES), const)],
        out_specs=[head_spec, head_spec,
                   pl.BlockSpec((1, ATT_HEADS, V_ROWS, tb), lambda i: (i // bps, 0, 0, i % bps)),
                   pl.BlockSpec((tb, 2 * GMLP_WIDTH), row),
                   pl.BlockSpec((tb, 3 * CONV_WIDTH), row),
                   pl.BlockSpec((tb, LANES), row),
                   pl.BlockSpec((tb, LANES), row)],
        out_shape=[head_shape, head_shape,
                   jax.ShapeDtypeStruct((B, ATT_HEADS, V_ROWS, S), bf16),
                   jax.ShapeDtypeStruct((T, 2 * GMLP_WIDTH), f32),
                   jax.ShapeDtypeStruct((T, 3 * CONV_WIDTH), f32),
                   jax.ShapeDtypeStruct((T, LANES), f32),
                   jax.ShapeDtypeStruct((T, LANES), f32)],
        compiler_params=_cparams(("parallel",)),
        name="in_proj",
    )(x2, g, w, wf, bfp)


def _in_proj_weights(wl):
    f0 = 3 * ATT_WIDTH

    def per_head(cols):
        w3 = cols.reshape(D_MODEL, ATT_HEADS, HEAD_DIM)
        return jnp.pad(w3, ((0, 0), (0, 0), (0, LANES - HEAD_DIM))).reshape(D_MODEL, -1)

    w_main = jnp.concatenate([per_head(wl[:, :ATT_WIDTH]), per_head(wl[:, ATT_WIDTH:2 * ATT_WIDTH]),
                              wl[:, 2 * ATT_WIDTH:f0], wl[:, f0 + ATT_HEADS:]], axis=1)
    wf = jnp.pad(wl[:, f0:f0 + ATT_HEADS], ((0, 0), (0, LANES - ATT_HEADS)))
    return w_main.astype(bf16), wf


def _cumsum_kernel(x_ref, kp_ref, ka_ref, carry_ref, *, tc):
    @pl.when(pl.program_id(1) == 0)
    def _():
        carry_ref[...] = jnp.zeros_like(carry_ref)

    row = lax.broadcasted_iota(jnp.int32, (tc, tc), 0)
    col = lax.broadcasted_iota(jnp.int32, (tc, tc), 1)
    tri = jnp.where(col <= row, 1.0, 0.0).astype(bf16)
    hi, mid, lo = _split3(x_ref[...])
    c = (jnp.dot(tri, hi, preferred_element_type=f32)
         + jnp.dot(tri, mid, preferred_element_type=f32)
         + jnp.dot(tri, lo, preferred_element_type=f32)) + carry_ref[...]
    carry_ref[...] = c[tc - 1:tc, :]
    hw = ATT_HEADS * LANES
    src = lax.broadcasted_iota(jnp.int32, (LANES, hw), 0)
    dst = lax.broadcasted_iota(jnp.int32, (LANES, hw), 1)
    fcols = jnp.zeros((tc, hw), f32)
    for p, part in enumerate(_split3(c * LOG2E)):
        place = jnp.where(dst == src * LANES + HEAD_DIM + p, 1.0, 0.0).astype(bf16)
        fcols = fcols + jnp.dot(part, place, preferred_element_type=f32)
    for hd in range(ATT_HEADS):
        ka_ref[0, hd] = (kp_ref[0, hd].astype(f32)
                         + fcols[:, hd * LANES:(hd + 1) * LANES]).astype(bf16)


def _cumsum(logf, kp, tc):
    B, H, S, _ = kp.shape
    nb = S // tc
    head_spec = pl.BlockSpec((1, H, tc, LANES), lambda b, j: (b, 0, j, 0))
    return pl.pallas_call(
        functools.partial(_cumsum_kernel, tc=tc),
        grid=(B, nb),
        in_specs=[pl.BlockSpec((tc, LANES), lambda b, j: (b * nb + j, 0)), head_spec],
        out_specs=head_spec,
        out_shape=jax.ShapeDtypeStruct(kp.shape, bf16),
        scratch_shapes=[pltpu.VMEM((1, LANES), f32)],
        compiler_params=_cparams(("arbitrary", "arbitrary")),
        name="cumsum_logf",
    )(logf, kp)


def _attn_kernel(jstart_ref, qa_ref, ka_ref, vt_ref, o_ref, acc_ref, m_ref, *, tq, hg):
    i = pl.program_id(2)
    m_ref[...] = jnp.full(m_ref.shape, NEG_BIG, f32)
    acc_ref[...] = jnp.zeros(acc_ref.shape, f32)

    def step(j, masked):
        start = pl.multiple_of(j * tq, tq)

        def scores(h):
            k = ka_ref[0, h, pl.ds(start, tq), :]
            s = lax.dot_general(k, qa_ref[0, h], _NT, preferred_element_type=f32)
            if masked:
                r = lax.broadcasted_iota(jnp.int32, (tq, tq), 0)
                c = lax.broadcasted_iota(jnp.int32, (tq, tq), 1)
                s = jnp.where(r <= c, s, NEG_BIG)
            return s

        def new_max(h, s):
            return jnp.maximum(m_ref[h], jnp.max(s, axis=0, keepdims=True))

        def weights(s, m_new):
            return jnp.exp2(s - m_new).astype(bf16)

        def accumulate(h, p, m_new):
            pv = jnp.dot(vt_ref[0, h, :, pl.ds(start, tq)], p, preferred_element_type=f32)
            acc_ref[h] = jnp.exp2(m_ref[h] - m_new) * acc_ref[h] + pv
            m_ref[h] = m_new

        s, mx, p = {}, {}, {}
        for t in range(hg + 3):
            if t < hg:
                s[t] = scores(t)
            if 0 <= t - 1 < hg:
                mx[t - 1] = new_max(t - 1, s[t - 1])
            if 0 <= t - 2 < hg:
                p[t - 2] = weights(s.pop(t - 2), mx[t - 2])
            if 0 <= t - 3 < hg:
                accumulate(t - 3, p.pop(t - 3), mx.pop(t - 3))

    def body(j, carry):
        step(j, False)
        return carry

    lax.fori_loop(jstart_ref[pl.program_id(0), pl.program_id(1), i], i, body, 0)
    step(i, True)
    for h in range(hg):
        o_ref[0, h * HEAD_DIM:(h + 1) * HEAD_DIM, :] = (
            acc_ref[h, :HEAD_DIM, :] / acc_ref[h, HEAD_DIM:HEAD_DIM + 1, :])


def _attention(jstart, qa, ka, vt, tq, hg):
    B, H, S, _ = qa.shape
    return pl.pallas_call(
        functools.partial(_attn_kernel, tq=tq, hg=hg),
        grid_spec=pltpu.PrefetchScalarGridSpec(
            num_scalar_prefetch=1,
            grid=(B, H // hg, S // tq),
            in_specs=[pl.BlockSpec((1, hg, tq, LANES), lambda b, g, i, js: (b, g, i, 0)),
                      pl.BlockSpec((1, hg, S, LANES), lambda b, g, i, js: (b, g, 0, 0),
                                   pipeline_mode=pl.Buffered(1)),
                      pl.BlockSpec((1, hg, V_ROWS, S), lambda b, g, i, js: (b, g, 0, 0),
                                   pipeline_mode=pl.Buffered(1))],
            out_specs=pl.BlockSpec((1, hg * HEAD_DIM, tq), lambda b, g, i, js: (b, g, i)),
            scratch_shapes=[pltpu.VMEM((hg, V_ROWS, tq), f32), pltpu.VMEM((hg, 1, tq), f32)]),
        out_shape=jax.ShapeDtypeStruct((B, H * HEAD_DIM, S), f32),
        compiler_params=_cparams(("parallel", "parallel", "arbitrary")),
        name="fox_attention",
    )(jstart, qa, ka, vt)


def _first_key_block(nrm, ka, tq, hg):
    B, H, S, _ = ka.shape
    nq = S // tq
    n2 = jnp.max(nrm.reshape(B, nq, tq, LANES), axis=2)
    qmax = jnp.sqrt(n2[:, :, :H]).transpose(0, 2, 1) * NORM_SLACK
    kmax = jnp.sqrt(n2[:, :, H:2 * H]).transpose(0, 2, 1) * NORM_SLACK
    blk = jnp.arange(nq, dtype=jnp.int32)
    earlier = blk[None, :] <= blk[:, None]
    kcum = jnp.max(jnp.where(earlier[None, None], kmax[:, :, None, :], 0.0), axis=3)
    fl = ka[:, :, :, HEAD_DIM:HEAD_DIM + F_PARTS].astype(f32).sum(-1)
    f_end = fl[:, :, tq - 1::tq]
    f_start = fl[:, :, ::tq]
    upper = qmax[:, :, :, None] * kcum[:, :, None, :] - f_end[:, :, None, :]
    lower = -qmax * kmax - f_start
    skip = upper < lower[:, :, :, None] - UNDERFLOW_LOG2
    lead = jnp.min(jnp.where(skip, nq, blk[None, None, None, :]), axis=3)
    lead = jnp.minimum(lead, blk[None, None, :])
    return lead.reshape(B, H // hg, hg, nq).min(axis=2).astype(jnp.int32)


def _mixout_kernel(oatt_ref, gm_ref, cv_ref, cvp_ref, x_ref, sgug_ref, wsp_ref, bsp_ref,
                   cw_ref, mg_ref, wout_ref, n2g_ref, xo_ref, h2_ref, *, tb, blocks_per_seq):
    i = pl.program_id(0)
    gm = gm_ref[...]
    u = _gelu(gm[:, :GMLP_WIDTH])
    v = _gelu(gm[:, GMLP_WIDTH:])
    gr = lax.broadcasted_iota(jnp.int32, (GMLP_WIDTH, GMLP_WIDTH), 0) // GROUP_DIM
    gc = lax.broadcasted_iota(jnp.int32, (GMLP_WIDTH, GMLP_WIDTH), 1) // GROUP_DIM
    grp = jnp.where(gr == gc, 1.0 / GROUP_DIM, 0.0).astype(bf16)
    vh, vm, vl = _split3(v * v)
    ms = (jnp.dot(vh, grp, preferred_element_type=f32)
          + jnp.dot(vm, grp, preferred_element_type=f32)
          + jnp.dot(vl, grp, preferred_element_type=f32))
    vn = (v * lax.rsqrt(ms + EPS) * sgug_ref[...]).astype(bf16)
    tr = lax.broadcasted_iota(jnp.int32, (CHUNK, CHUNK), 0)
    tc = lax.broadcasted_iota(jnp.int32, (CHUNK, CHUNK), 1)
    lane_g = lax.broadcasted_iota(jnp.int32, (CHUNK, GMLP_WIDTH), 1) // GROUP_DIM
    mixed_chunks = []
    for c in range(tb // CHUNK):
        vc = vn[c * CHUNK:(c + 1) * CHUNK, :]
        mixed = jnp.zeros((CHUNK, GMLP_WIDTH), f32)
        for g in range(GMLP_GROUPS):
            wg = jnp.where(tc <= tr, wsp_ref[g], 0.0).astype(bf16)
            mg = jnp.dot(wg, vc, preferred_element_type=f32)
            mixed = jnp.where(lane_g == g, mg, mixed)
        mixed_chunks.append(mixed + bsp_ref[...])
    o_gmlp = u * jnp.concatenate(mixed_chunks, axis=0)
    cv = cv_ref[...]
    c_in = cv[:, :CONV_WIDTH]
    c_b = cv[:, CONV_WIDTH:2 * CONV_WIDTH]
    c_c = cv[:, 2 * CONV_WIDTH:]
    z = c_c * c_in
    cvp = cvp_ref[...]
    zp = cvp[:, 2 * CONV_WIDTH:] * cvp[:, :CONV_WIDTH]
    zp = jnp.where(i % blocks_per_seq == 0, jnp.zeros_like(zp), zp)
    rowi = lax.broadcasted_iota(jnp.int32, (tb, CONV_WIDTH), 0)
    z1 = pltpu.roll(z, 1, axis=0)
    z1 = jnp.where(rowi == 0, zp[7:8, :], z1)
    z2 = pltpu.roll(z, 2, axis=0)
    z2 = jnp.where(rowi == 0, zp[6:7, :], z2)
    z2 = jnp.where(rowi == 1, zp[7:8, :], z2)
    cw = cw_ref[...]
    o_conv = c_b * (cw[0:1, :] * z2 + cw[1:2, :] * z1 + cw[2:3, :] * z)
    mg_ = mg_ref[...]
    o_att = oatt_ref[0].T
    mixed_all = jnp.concatenate([
        _rms(o_att, mg_[:, :ATT_WIDTH]),
        _rms(o_gmlp, mg_[:, ATT_WIDTH:ATT_WIDTH + GMLP_WIDTH]),
        _rms(o_conv, mg_[:, ATT_WIDTH + GMLP_WIDTH:])], axis=-1).astype(bf16)
    xn = x_ref[...] + jnp.dot(mixed_all, wout_ref[...], preferred_element_type=f32)
    xo_ref[...] = xn
    h2_ref[...] = _rms(xn, n2g_ref[...]).T.astype(bf16)


def _mix_out(oatt_t, gm, cv, x2, sgug, wsp, bspf, cw, mg, wout, n2g, S, tb):
    T = x2.shape[0]
    bps = S // tb
    const2 = lambda i: (0, 0)
    row = lambda i: (i, 0)
    return pl.pallas_call(
        functools.partial(_mixout_kernel, tb=tb, blocks_per_seq=bps),
        grid=(T // tb,),
        in_specs=[pl.BlockSpec((1, ATT_WIDTH, tb), lambda i: (i // bps, 0, i % bps)),
                  pl.BlockSpec((tb, 2 * GMLP_WIDTH), row),
                  pl.BlockSpec((tb, 3 * CONV_WIDTH), row),
                  pl.BlockSpec((8, 3 * CONV_WIDTH),
                               lambda i: (jnp.maximum(i * (tb // 8) - 1, 0), 0)),
                  pl.BlockSpec((tb, D_MODEL), row),
                  pl.BlockSpec((1, GMLP_WIDTH), const2),
                  pl.BlockSpec((GMLP_GROUPS, CHUNK, CHUNK), lambda i: (0, 0, 0)),
                  pl.BlockSpec((CHUNK, GMLP_WIDTH), const2),
                  pl.BlockSpec((8, CONV_WIDTH), const2),
                  pl.BlockSpec((1, MIX_WIDTH), const2),
                  pl.BlockSpec((MIX_WIDTH, D_MODEL), const2),
                  pl.BlockSpec((1, D_MODEL), const2)],
        out_specs=[pl.BlockSpec((tb, D_MODEL), row),
                   pl.BlockSpec((D_MODEL, tb), lambda i: (0, i))],
        out_shape=[jax.ShapeDtypeStruct((T, D_MODEL), f32),
                   jax.ShapeDtypeStruct((D_MODEL, T), bf16)],
        compiler_params=_cparams(("parallel",)),
        name="mix_out",
    )(oatt_t, gm, cv, cv, x2, sgug, wsp, bspf, cw, mg, wout, n2g)


def _colmax(x):
    return jnp.max(x, axis=0, keepdims=True)


def _colsum(x):
    return jnp.sum(x, axis=0, keepdims=True)


def _tree(op, xs):
    while len(xs) > 1:
        xs = [op(xs[i], xs[i + 1]) if i + 1 < len(xs) else xs[i] for i in range(0, len(xs), 2)]
    return xs[0]


def _top16(scores, vals_refs):
    scores = list(scores)
    for a in range(PEER_TOPK):
        for i, s in enumerate(scores):
            m = _colmax(s)
            scores[i] = jnp.where(s == m, RANK_MARK * (1.0 + a * RANK_STEP), s)
            vals_refs[i][a:a + 1, :] = m
    ranks = []
    for s in scores:
        rank = jnp.round((s - RANK_MARK) * (1.0 / (RANK_MARK * RANK_STEP)))
        ranks.append(jnp.where(s <= RANK_MARK, rank, float(PEER_TOPK)))
    return ranks


def _select_tables(s1, s2, v1_ref, v2_ref):
    r1, r2 = _top16([s1, s2], [v1_ref, v2_ref])
    v1 = v1_ref[...]
    v2 = v2_ref[...]
    row8 = lax.broadcasted_iota(jnp.int32, (8, s1.shape[1]), 0)
    v2lo = v2[0:8]

    def packed(segments):
        arr, off = None, 0
        for a, nb in segments:
            src = v1[a:a + 1] + (v2lo if off == 0 else pltpu.roll(v2lo, off, axis=0))
            arr = src if arr is None else jnp.where(row8 >= off, src, arr)
            off += nb
        return arr if off == 8 else jnp.where(row8 >= off, -jnp.inf, arr)

    cands = [v1[0:1] + v2lo, v1[0:1] + v2[8:16], v1[1:2] + v2lo, v1[8:16] + v2[0:1]]
    cands += [packed(seg) for seg in CAND_PACKING]
    cmax = v1[0:1] + v2[0:1]
    work = list(cands)
    tau = cmax
    cnt = jnp.zeros_like(cmax)
    for _ in range(PEER_TOPK):
        m = _colmax(_tree(jnp.maximum, work))
        tau = jnp.where(cnt < float(PEER_TOPK), m, tau)
        hits = [w == m for w in work]
        cnt = cnt + _colsum(_tree(jnp.add, [jnp.where(h, 1.0, 0.0) for h in hits]))
        work = [jnp.where(h, -jnp.inf, w) for h, w in zip(hits, work)]
    sel = [c >= tau for c in cands]
    zsum = _colsum(_tree(jnp.add, [jnp.where(s_, jnp.exp(c - cmax), 0.0)
                                   for s_, c in zip(sel, cands)]))
    ones = [jnp.where(s_, 1.0, 0.0) for s_ in sel]
    n_lo = {0: _colsum(ones[0] + ones[1]), 1: _colsum(ones[2])}
    for o, segments in zip(ones[4:], CAND_PACKING):
        off = 0
        for a, nb in segments:
            inside = jnp.where(row8 >= off, jnp.where(row8 < off + nb, o, 0.0), 0.0)
            n_lo[a] = _colsum(inside)
            off += nb
    n_hi = ones[3]
    n1 = jnp.zeros(r1.shape, f32)
    for a in range(8):
        n1 = jnp.where(r1 == float(a), n_lo[a], n1)
    for a in range(8, PEER_TOPK):
        n1 = jnp.where(r1 == float(a), n_hi[a - 8:a - 7], n1)
    e2 = jnp.exp(s2 - v2[0:1])
    c1 = jnp.exp(s1 - v1[0:1]) / zsum
    return r2.astype(bf16), e2.astype(bf16), n1, c1


def _peerprep_kernel(h2_ref, wq_ref, keys_ref, r2_ref, e2_ref, n1_ref, c1_ref,
                     v1_ref, v2_ref):
    def scores(c):
        qT = jnp.dot(wq_ref[...], h2_ref[:, c * SLAB:(c + 1) * SLAB],
                     preferred_element_type=f32)
        qb = qT.astype(bf16)
        return (jnp.dot(keys_ref[0], qb[:D_HALF], preferred_element_type=f32),
                jnp.dot(keys_ref[1], qb[D_HALF:], preferred_element_type=f32))

    lpc = SLAB // LANES
    nslab = r2_ref.shape[1]
    ahead = scores(0)
    for c in range(nslab):
        s1, s2 = ahead
        if c + 1 < nslab:
            ahead = scores(c + 1)
        r2s, e2s = [], []
        for k in range(lpc):
            t = c * lpc + k
            ls = slice(k * LANES, (k + 1) * LANES)
            r2, e2, n1, c1 = _select_tables(s1[:, ls], s2[:, ls], v1_ref, v2_ref)
            r2s.append(r2)
            e2s.append(e2)
            n1_ref[0, t] = n1
            c1_ref[0, t] = c1
        r2_ref[0, c] = jnp.concatenate(r2s, axis=1)
        e2_ref[0, c] = jnp.concatenate(e2s, axis=1)


def _peer_prep(h2t, wqT, keys, tb):
    T = h2t.shape[1]
    spec = pl.BlockSpec((1, tb // SLAB, N_KEYS, SLAB), lambda t, h: (h, t, 0, 0))
    shp = (PEER_HEADS, T // SLAB, N_KEYS, SLAB)
    rspec = pl.BlockSpec((1, tb // LANES, N_KEYS, LANES), lambda t, h: (h, t, 0, 0))
    rshp = (PEER_HEADS, T // LANES, N_KEYS, LANES)
    return pl.pallas_call(
        _peerprep_kernel,
        grid=(T // tb, PEER_HEADS),
        in_specs=[pl.BlockSpec((D_MODEL, tb), lambda t, h: (0, t)),
                  pl.BlockSpec((D_KEY, D_MODEL), lambda t, h: (h, 0)),
                  pl.BlockSpec((2, N_KEYS, D_HALF), lambda t, h: (0, 0, 0))],
        out_specs=[spec, spec, rspec, rspec],
        out_shape=[jax.ShapeDtypeStruct(shp, bf16), jax.ShapeDtypeStruct(shp, bf16),
                   jax.ShapeDtypeStruct(rshp, f32), jax.ShapeDtypeStruct(rshp, f32)],
        scratch_shapes=[pltpu.VMEM((PEER_TOPK, LANES), f32), pltpu.VMEM((PEER_TOPK, LANES), f32)],
        compiler_params=_cparams(("parallel", "arbitrary")),
        name="peer_prep",
    )(h2t, wqT, keys)


def _peermain_kernel(h2_ref, u_ref, vt_ref, r2_ref, e2_ref, n1_ref, c1_ref, x_ref, fg_ref,
                     o_ref, acc_ref, *, rows, final):
    e = pl.program_id(1)
    tb = h2_ref.shape[1]
    nc = tb // SLAB
    lpc = SLAB // LANES

    @pl.when(e == 0)
    def _():
        acc_ref[...] = jnp.zeros_like(acc_ref)

    base = e * rows

    def row_tile(ref, h, r, c):
        row = jnp.concatenate(
            [jnp.broadcast_to(ref[h, c * lpc + k, pl.ds(base + r, 1), :], (BF16_ROWS, LANES))
             for k in range(lpc)], axis=1).astype(bf16)
        return jnp.tile(row, (GATE_ROWS // BF16_ROWS, 1))

    zero = jnp.zeros((GATE_ROWS, SLAB), bf16)

    ub = UNIT_ROWS * N_KEYS
    units = [(c, u) for c in range(nc) for u in range(rows // UNIT_ROWS)]

    def activations(unit):
        c, u = unit
        return jnp.dot(u_ref[u * ub:(u + 1) * ub, :], h2_ref[:, c * SLAB:(c + 1) * SLAB],
                       preferred_element_type=f32)

    def gates(unit):
        c, u = unit
        out = []
        for r in range(u * UNIT_ROWS, (u + 1) * UNIT_ROWS):
            for q in range(N_KEYS // GATE_ROWS):
                qs = slice(q * GATE_ROWS, (q + 1) * GATE_ROWS)
                g = None
                for h in range(PEER_HEADS):
                    term = jnp.where(r2_ref[h, c, qs, :] < row_tile(n1_ref, h, r, c),
                                     e2_ref[h, c, qs, :], zero)
                    term = term * row_tile(c1_ref, h, r, c)
                    g = term if g is None else g + term
                out.append(g)
        return jnp.concatenate(out, axis=0)

    upc = rows // UNIT_ROWS

    def accumulate(c, w):
        cs = slice(c * SLAB, (c + 1) * SLAB)
        acc_ref[:, cs] += jnp.dot(vt_ref[...], w, preferred_element_type=f32)

    a, g, w = {}, {}, {}
    for t in range(len(units) + 3):
        if t < len(units):
            a[t] = activations(units[t])
        if 0 <= t - 1 < len(units):
            g[t - 1] = gates(units[t - 1])
        if 0 <= t - 2 < len(units):
            w[t - 2] = _gelu(a.pop(t - 2).astype(bf16)) * g.pop(t - 2)
        if 0 <= t - 3 < len(units) and (t - 3) % upc == upc - 1:
            c = (t - 3) // upc
            accumulate(c, jnp.concatenate([w.pop(c * upc + k) for k in range(upc)], axis=0))

    @pl.when(e == pl.num_programs(1) - 1)
    def _():
        y = x_ref[...] + acc_ref[...].T
        o_ref[...] = _rms(y, fg_ref[...]) if final else y


def _peer_main(h2t, u, vt, r2, e2, n1, c1, x2, fg, layer, tb, rows, final):
    T = h2t.shape[1]
    eb = rows * N_KEYS
    sel = pl.BlockSpec((PEER_HEADS, tb // SLAB, N_KEYS, SLAB), lambda t, e: (0, t, 0, 0))
    rsel = pl.BlockSpec((PEER_HEADS, tb // LANES, N_KEYS, LANES), lambda t, e: (0, t, 0, 0))
    return pl.pallas_call(
        functools.partial(_peermain_kernel, rows=rows, final=final),
        grid=(T // tb, N_KEYS // rows),
        in_specs=[pl.BlockSpec((D_MODEL, tb), lambda t, e: (0, t)),
                  pl.BlockSpec((None, eb, D_MODEL), lambda t, e: (layer, e, 0)),
                  pl.BlockSpec((None, D_MODEL, eb), lambda t, e: (layer, 0, e)),
                  sel, sel, rsel, rsel,
                  pl.BlockSpec((tb, D_MODEL), lambda t, e: (t, 0)),
                  pl.BlockSpec((1, D_MODEL), lambda t, e: (0, 0))],
        out_specs=pl.BlockSpec((tb, D_MODEL), lambda t, e: (t, 0)),
        out_shape=jax.ShapeDtypeStruct((T, D_MODEL), f32),
        scratch_shapes=[pltpu.VMEM((D_MODEL, tb), f32)],
        compiler_params=_cparams(("parallel", "arbitrary")),
        name="peer_main",
    )(h2t, u, vt, r2, e2, n1, c1, x2, fg)


def _block_sizes(S):
    return dict(tb_in=min(1024, S), tc=min(512, S), tq=min(512, S), heads_per_step=8,
                tb_mix=min(512, S), tb_prep=min(2048, S), tb_peer=min(512, S), rows=16)


def kernel(x, norm1_g, w_in, b_forget, sgu_norm_g, w_spatial, b_spatial, conv_w,
           mix_norm_g, w_out, norm2_g, w_query, sub_keys, expert_u, expert_v, final_g):
    B, S, D = x.shape
    T = B * S
    depth = w_in.shape[0]
    bs = _block_sizes(S)
    x2 = x.reshape(T, D)
    u_all = expert_u.astype(bf16)
    vt_all = expert_v.transpose(0, 2, 1).astype(bf16)
    for l in range(depth):
        w_main, wf = _in_proj_weights(w_in[l])
        bfp = jnp.pad(b_forget[l], (0, LANES - ATT_HEADS)).reshape(1, LANES)
        qa, kp, vt, gm, cv, logf, nrm = _in_proj(x2, norm1_g[l].reshape(1, D), w_main, wf, bfp,
                                                 B, S, bs["tb_in"])
        ka = _cumsum(logf, kp, bs["tc"])
        jstart = _first_key_block(nrm, ka, bs["tq"], bs["heads_per_step"])
        oatt_t = _attention(jstart, qa, ka, vt, bs["tq"], bs["heads_per_step"])
        bspf = jnp.repeat(b_spatial[l].T, GROUP_DIM, axis=1)
        cw = jnp.pad(conv_w[l], ((0, 8 - CONV_K), (0, 0)))
        x2, h2t = _mix_out(oatt_t, gm, cv, x2, sgu_norm_g[l].reshape(1, GMLP_WIDTH),
                          w_spatial[l], bspf, cw, mix_norm_g[l].reshape(1, MIX_WIDTH),
                          w_out[l].astype(bf16), norm2_g[l].reshape(1, D), S, bs["tb_mix"])
        r2, e2, n1, c1 = _peer_prep(h2t, w_query[l].T.astype(bf16), sub_keys[l].astype(bf16),
                                    bs["tb_prep"])
        x2 = _peer_main(h2t, u_all, vt_all, r2, e2, n1, c1, x2, final_g.reshape(1, D), l,
                        bs["tb_peer"], bs["rows"], final=(l == depth - 1))
    return x2.reshape(B, S, D)
```
